```python
import math
import jax, jax.numpy as jnp
from jax import lax
import numpy as np

D_MODEL = 4096
BATCH = 2
SEQ = 8192
DEPTH = 1

CHUNK = 64
Q_BLOCK = 128
EPS = 1e-6
ROPE_THETA = 500000.0

DA_HEADS = 8
DA_QK_DIM = D_MODEL // 32
DA_V_DIM = 2 * DA_QK_DIM
DA_WIDTH = DA_HEADS * DA_V_DIM
DA_QK_WIDTH = DA_HEADS * 2 * DA_QK_DIM
ROPE_DIM = DA_QK_DIM // 4

POOL_WINDOWS = (2, 4, 8, 16)
POOL_GROUPS = len(POOL_WINDOWS)
POOL_WIDTH = D_MODEL // 4
POOL_GDIM = POOL_WIDTH // POOL_GROUPS

MEM_LEN = 256
MEM_HEADS = 4
MEM_HEAD_DIM = D_MODEL // 16
MEM_WIDTH = MEM_HEADS * MEM_HEAD_DIM

N_BRANCHES = 3
IN_WIDTH = 3 * DA_QK_WIDTH + POOL_WIDTH + MEM_WIDTH + N_BRANCHES * D_MODEL

D_FF = ((8 * D_MODEL // 3 + 255) // 256) * 256
CONV_W = 3

kernel_name = "hybrid_diffattn_pool_mem_convffn"


def rmsnorm(x, g):
    xf = x.astype(jnp.float32)
    y = xf * lax.rsqrt(jnp.mean(xf * xf, axis=-1, keepdims=True) + EPS)
    return y.astype(x.dtype) * g


def rope_tables(seq):
    pos = jnp.arange(seq, dtype=jnp.float32)
    inv = ROPE_THETA ** (-jnp.arange(0, ROPE_DIM, 2, dtype=jnp.float32) / ROPE_DIM)
    ang = pos[:, None] * inv[None, :]
    return jnp.cos(ang), jnp.sin(ang)


def partial_rope(x, cos, sin):
    half = ROPE_DIM // 2
    c = cos[:, None, None, :].astype(x.dtype)
    s = sin[:, None, None, :].astype(x.dtype)
    x1 = x[..., :half]
    x2 = x[..., half:ROPE_DIM]
    rot = jnp.concatenate([x1 * c - x2 * s, x2 * c + x1 * s], axis=-1)
    return jnp.concatenate([rot, x[..., ROPE_DIM:]], axis=-1)


def diff_attention(q, k, v, lam, g_subln, lambda_init):
    b, s = q.shape[0], q.shape[1]
    nblk = s // Q_BLOCK
    scale = DA_QK_DIM ** -0.5
    key_chunk = jnp.arange(s) // CHUNK
    qb = q.reshape(b, nblk, Q_BLOCK, DA_HEADS, 2, DA_QK_DIM).transpose(1, 0, 2, 3, 4, 5)
    neg = jnp.finfo(jnp.float32).min

    def block(args):
        qblk, i = args
        q_chunk = (i * Q_BLOCK + jnp.arange(Q_BLOCK)) // CHUNK
        allowed = key_chunk[None, :] <= q_chunk[:, None]
        sc = jnp.einsum('bqhcd,bkhcd->bhcqk', qblk, k).astype(jnp.float32) * scale
        sc = jnp.where(allowed, sc, neg)
        p = jax.nn.softmax(sc, axis=-1)
        a = p[:, :, 0] - lam * p[:, :, 1]
        return jnp.einsum('bhqk,bkhd->bqhd', a.astype(v.dtype), v)

    o = lax.map(block, (qb, jnp.arange(nblk)))
    o = o.transpose(1, 0, 2, 3, 4).reshape(b, s, DA_HEADS, DA_V_DIM)
    o = rmsnorm(o, g_subln) * (1.0 - lambda_init)
    return o.reshape(b, s, DA_WIDTH)


def pool_mixer(u, w_pool, pool_scale):
    b, s = u.shape[0], u.shape[1]
    ug = u.reshape(b, s, POOL_GROUPS, POOL_GDIM)
    cs = jnp.cumsum(ug.astype(jnp.float32), axis=1)
    cs_pad = jnp.pad(cs, ((0, 0), (1, 0), (0, 0), (0, 0)))
    win = jnp.array(POOL_WINDOWS, dtype=jnp.int32)
    t = jnp.arange(s, dtype=jnp.int32)[:, None]
    lo = jnp.maximum(t + 1 - win[None, :], 0)
    gidx = jnp.arange(POOL_GROUPS)[None, :]
    lower = cs_pad[:, lo, gidx, :]
    cnt = jnp.minimum(t + 1, win[None, :]).astype(jnp.float32)
    mean = (cs_pad[:, 1:] - lower) / cnt[None, :, :, None]
    y = (mean - ug.astype(jnp.float32)).astype(u.dtype)
    y = jnp.einsum('bsgi,gio->bsgo', y, w_pool) * pool_scale
    return y.reshape(b, s, POOL_WIDTH)


def mem_attention(qm, mem, g_mem, w_mkv, g_qm, g_km):
    b, s = qm.shape[0], qm.shape[1]
    q = rmsnorm(qm.reshape(b, s, MEM_HEADS, MEM_HEAD_DIM), g_qm)
    kv = (rmsnorm(mem, g_mem) @ w_mkv).reshape(b, mem.shape[1], 2, MEM_HEADS, MEM_HEAD_DIM)
    k = rmsnorm(kv[:, :, 0], g_km)
    v = kv[:, :, 1]
    sc = jnp.einsum('bshd,bmhd->bhsm', q, k).astype(jnp.float32) * (MEM_HEAD_DIM ** -0.5)
    p = jax.nn.softmax(sc, axis=-1).astype(v.dtype)
    o = jnp.einsum('bhsm,bmhd->bshd', p, v)
    return o.reshape(b, s, MEM_WIDTH)


def conv_ffn(h, w_up, conv_w, conv_b, w_down):
    u = h @ w_up
    up = jnp.pad(u, ((0, 0), (CONV_W - 1, 0), (0, 0)))
    s = u.shape[1]
    u = up[:, 0:s] * conv_w[0] + up[:, 1:s + 1] * conv_w[1] + up[:, 2:s + 2] * conv_w[2] + conv_b
    gate, val = jnp.split(u, 2, axis=-1)
    return (jax.nn.silu(gate) * val) @ w_down


def setup_inputs(seed: int = 0) -> dict:
    key = jax.random.key(seed)
    ks = jax.random.split(key, 32)
    f32 = jnp.float32
    nrm = lambda k, shape, scale: jax.random.normal(k, shape, f32) * scale
    gain = lambda k, shape: 1.0 + 0.1 * jax.random.normal(k, shape, f32)
    L = DEPTH
    return {
        "x": nrm(ks[0], (BATCH, SEQ, D_MODEL), 1.0),
        "mem": nrm(ks[1], (BATCH, MEM_LEN, D_MODEL), 1.0),
        "g_attn_norm": gain(ks[2], (L, D_MODEL)),
        "w_in": nrm(ks[3], (L, D_MODEL, IN_WIDTH), D_MODEL ** -0.5),
        "g_qa": gain(ks[4], (L, DA_QK_DIM)),
        "g_ka": gain(ks[5], (L, DA_QK_DIM)),
        "lam_q1": nrm(ks[6], (L, DA_QK_DIM), 0.1),
        "lam_k1": nrm(ks[7], (L, DA_QK_DIM), 0.1),
        "lam_q2": nrm(ks[8], (L, DA_QK_DIM), 0.1),
        "lam_k2": nrm(ks[9], (L, DA_QK_DIM), 0.1),
        "g_subln": gain(ks[10], (L, DA_V_DIM)),
        "w_pool": nrm(ks[11], (L, POOL_GROUPS, POOL_GDIM, POOL_GDIM), POOL_GDIM ** -0.5),
        "pool_scale": gain(ks[12], (L, POOL_GROUPS, POOL_GDIM)),
        "g_mem": gain(ks[13], (L, D_MODEL)),
        "w_mkv": nrm(ks[14], (L, D_MODEL, 2 * MEM_WIDTH), D_MODEL ** -0.5),
        "g_qm": gain(ks[15], (L, MEM_HEAD_DIM)),
        "g_km": gain(ks[16], (L, MEM_HEAD_DIM)),
        "w_a": nrm(ks[17], (L, DA_WIDTH, D_MODEL), DA_WIDTH ** -0.5),
        "w_b": nrm(ks[18], (L, POOL_WIDTH, D_MODEL), POOL_WIDTH ** -0.5),
        "w_c": nrm(ks[19], (L, MEM_WIDTH, D_MODEL), MEM_WIDTH ** -0.5),
        "w_o": nrm(ks[20], (L, D_MODEL, D_MODEL), D_MODEL ** -0.5),
        "g_ffn_norm": gain(ks[21], (L, D_MODEL)),
        "w_up": nrm(ks[22], (L, D_MODEL, 2 * D_FF), D_MODEL ** -0.5),
        "conv_w": nrm(ks[23], (L, CONV_W, 2 * D_FF), CONV_W ** -0.5),
        "conv_b": nrm(ks[24], (L, 2 * D_FF), 0.01),
        "w_down": nrm(ks[25], (L, D_FF, D_MODEL), D_FF ** -0.5),
    }


def reference(x, mem, g_attn_norm, w_in, g_qa, g_ka, lam_q1, lam_k1, lam_q2, lam_k2,
              g_subln, w_pool, pool_scale, g_mem, w_mkv, g_qm, g_km, w_a, w_b, w_c,
              w_o, g_ffn_norm, w_up, conv_w, conv_b, w_down):
    b, s = x.shape[0], x.shape[1]
    cos, sin = rope_tables(s)
    splits = np.cumsum([DA_QK_WIDTH, DA_QK_WIDTH, DA_QK_WIDTH, POOL_WIDTH, MEM_WIDTH]).tolist()
    for l in range(DEPTH):
        lambda_init = 0.8 - 0.6 * math.exp(-0.3 * l)
        h = rmsnorm(x, g_attn_norm[l])
        proj = h @ w_in[l]
        qa, ka, va, up, qm, gl = jnp.split(proj, splits, axis=-1)
        qa = partial_rope(rmsnorm(qa.reshape(b, s, DA_HEADS, 2, DA_QK_DIM), g_qa[l]), cos, sin)
        ka = partial_rope(rmsnorm(ka.reshape(b, s, DA_HEADS, 2, DA_QK_DIM), g_ka[l]), cos, sin)
        va = va.reshape(b, s, DA_HEADS, DA_V_DIM)
        lam = (jnp.exp(jnp.sum(lam_q1[l].astype(jnp.float32) * lam_k1[l].astype(jnp.float32)))
               - jnp.exp(jnp.sum(lam_q2[l].astype(jnp.float32) * lam_k2[l].astype(jnp.float32)))
               + lambda_init)
        ya = diff_attention(qa, ka, va, lam, g_subln[l], lambda_init)
        yb = pool_mixer(up, w_pool[l], pool_scale[l])
        yc = mem_attention(qm, mem, g_mem[l], w_mkv[l], g_qm[l], g_km[l])
        gates = jax.nn.sigmoid(gl.reshape(b, s, N_BRANCHES, D_MODEL))
        merged = (gates[:, :, 0] * (ya @ w_a[l]) + gates[:, :, 1] * (yb @ w_b[l])
                  + gates[:, :, 2] * (yc @ w_c[l]))
        x = x + merged @ w_o[l]
        x = x + conv_ffn(rmsnorm(x, g_ffn_norm[l]), w_up[l], conv_w[l], conv_b[l], w_down[l])
    return x
```

```python
import functools
import math

import jax
import jax.numpy as jnp
from jax import lax
from jax.experimental import pallas as pl
from jax.experimental.pallas import tpu as pltpu

F32 = jnp.float32
BF16 = jnp.bfloat16

EPS = 1e-6
ROPE_THETA = 500000.0
CHUNK = 64
POOL_WINDOWS = (2, 4, 8, 16)
CONV_W = 3

LANES = 128
SUBLANES = 8
VMEM_LIMIT_BYTES = 56 * 1024 * 1024


def _params(*sem):
    return pltpu.CompilerParams(dimension_semantics=sem, vmem_limit_bytes=VMEM_LIMIT_BYTES)


def _group_rmsnorm(x, g):
    ms = jnp.mean(x * x, axis=-1, keepdims=True)
    return x * lax.rsqrt(ms + EPS) * g


def _rmsnorm_kernel(x_ref, g_ref, o_ref):
    o_ref[...] = _group_rmsnorm(x_ref[...], g_ref[...]).astype(o_ref.dtype)


def _rmsnorm(x, g, tm):
    m, d = x.shape
    return pl.pallas_call(
        _rmsnorm_kernel,
        grid=(m // tm,),
        in_specs=[pl.BlockSpec((tm, d), lambda i: (i, 0)), pl.BlockSpec((1, d), lambda i: (0, 0))],
        out_specs=pl.BlockSpec((tm, d), lambda i: (i, 0)),
        out_shape=jax.ShapeDtypeStruct((m, d), BF16),
        compiler_params=_params("parallel"),
        name="rmsnorm",
    )(x, g.reshape(1, d))


def _proj_plain_kernel(x_ref, w_ref, o_ref):
    o_ref[...] = jnp.dot(x_ref[...], w_ref[...], preferred_element_type=F32).astype(o_ref.dtype)


def _proj_plain(x, w, col0, ncols, out_dtype, tm, tn, name):
    m, k = x.shape
    j0 = col0 // tn
    return pl.pallas_call(
        _proj_plain_kernel,
        grid=(m // tm, ncols // tn),
        in_specs=[pl.BlockSpec((tm, k), lambda i, j: (i, 0)),
                  pl.BlockSpec((k, tn), lambda i, j: (0, j0 + j))],
        out_specs=pl.BlockSpec((tm, tn), lambda i, j: (i, j)),
        out_shape=jax.ShapeDtypeStruct((m, ncols), out_dtype),
        compiler_params=_params("parallel", "parallel"),
        name=name,
    )(x, w)


def _proj_norm_kernel(x_ref, w_ref, g_ref, o_ref, *, gdim):
    acc = jnp.dot(x_ref[...], w_ref[...], preferred_element_type=F32)
    for c in range(acc.shape[1] // gdim):
        sl = slice(c * gdim, (c + 1) * gdim)
        o_ref[:, sl] = _group_rmsnorm(acc[:, sl], g_ref[...]).astype(o_ref.dtype)


def _proj_norm(x, w, g, col0, ncols, tm, tn, name):
    m, k = x.shape
    gdim = g.shape[0]
    j0 = col0 // tn
    return pl.pallas_call(
        functools.partial(_proj_norm_kernel, gdim=gdim),
        grid=(m // tm, ncols // tn),
        in_specs=[pl.BlockSpec((tm, k), lambda i, j: (i, 0)),
                  pl.BlockSpec((k, tn), lambda i, j: (0, j0 + j)),
                  pl.BlockSpec((1, gdim), lambda i, j: (0, 0))],
        out_specs=pl.BlockSpec((tm, tn), lambda i, j: (i, j)),
        out_shape=jax.ShapeDtypeStruct((m, ncols), BF16),
        compiler_params=_params("parallel", "parallel"),
        name=name,
    )(x, w, g.reshape(1, gdim))


def _proj_qk_kernel(x_ref, w_ref, g_ref, cos_ref, sa_ref, sb_ref, o_ref, *, rope_dim):
    acc = jnp.dot(x_ref[...], w_ref[...], preferred_element_type=F32)
    half = rope_dim // 2
    for c in range(acc.shape[1] // LANES):
        sl = slice(c * LANES, (c + 1) * LANES)
        y = _group_rmsnorm(acc[:, sl], g_ref[:, sl])
        y = (y * cos_ref[...] + pltpu.roll(y, LANES - half, 1) * sa_ref[...]
             + pltpu.roll(y, half, 1) * sb_ref[...])
        o_ref[:, sl] = y.astype(o_ref.dtype)


def _proj_qk(x, w, g_cols, cos_t, sa_t, sb_t, seq, rope_dim, tm, tn):
    m, k = x.shape
    ncols = g_cols.shape[0]
    nseq = seq // tm
    tab = pl.BlockSpec((tm, LANES), lambda i, j: (i % nseq, 0))
    return pl.pallas_call(
        functools.partial(_proj_qk_kernel, rope_dim=rope_dim),
        grid=(m // tm, ncols // tn),
        in_specs=[pl.BlockSpec((tm, k), lambda i, j: (i, 0)),
                  pl.BlockSpec((k, tn), lambda i, j: (0, j)),
                  pl.BlockSpec((1, tn), lambda i, j: (0, j)),
                  tab, tab, tab],
        out_specs=pl.BlockSpec((tm, tn), lambda i, j: (i, j)),
        out_shape=jax.ShapeDtypeStruct((m, ncols), BF16),
        compiler_params=_params("parallel", "parallel"),
        name="proj_qk",
    )(x, w, g_cols.reshape(1, ncols), cos_t, sa_t, sb_t)


def _diff_attn_kernel(q_ref, k_ref, v_ref, lam_ref, gs_ref, o_ref, m_scr, l_scr, acc_scr,
                      *, tq, tk, dk, lambda_init):
    i = pl.program_id(2)
    scale = dk ** -0.5
    neg = -1e30
    q = q_ref[...]
    m_scr[...] = jnp.full(m_scr.shape, neg, F32)
    l_scr[...] = jnp.zeros(l_scr.shape, F32)
    acc_scr[...] = jnp.zeros(acc_scr.shape, F32)

    def step(kv, masked):
        koff = pl.multiple_of(kv * tk, tk)
        k = k_ref[pl.ds(koff, tk), :]
        v = v_ref[pl.ds(koff, tk), :]
        if masked:
            rows = lax.broadcasted_iota(jnp.int32, (tq, tk), 0) // CHUNK
            cols = lax.broadcasted_iota(jnp.int32, (tq, tk), 1) // CHUNK
            allowed = cols <= rows
        for c in range(2):
            sl = slice(c * dk, (c + 1) * dk)
            s = lax.dot_general(q[:, sl], k[:, sl], (((1,), (1,)), ((), ())),
                                preferred_element_type=F32) * scale
            if masked:
                s = jnp.where(allowed, s, neg)
            m_prev = m_scr[c]
            m_new = jnp.maximum(m_prev, jnp.max(s, axis=-1, keepdims=True))
            alpha = jnp.exp(m_prev - m_new)
            p = jnp.exp(s - m_new)
            l_scr[c] = alpha * l_scr[c] + jnp.sum(p, axis=-1, keepdims=True)
            acc_scr[c] = alpha * acc_scr[c] + jnp.dot(p.astype(BF16), v, preferred_element_type=F32)
            m_scr[c] = m_new

    def body(kv, carry):
        step(kv, False)
        return carry

    lax.fori_loop(0, i, body, 0)
    step(i, True)

    lam_v = lam_ref[...]
    lam = (jnp.exp(jnp.sum(lam_v[0:1] * lam_v[1:2], axis=-1, keepdims=True))
           - jnp.exp(jnp.sum(lam_v[2:3] * lam_v[3:4], axis=-1, keepdims=True)) + lambda_init)
    o = acc_scr[0] / l_scr[0] - lam * (acc_scr[1] / l_scr[1])
    y = _group_rmsnorm(o, gs_ref[...]) * (1.0 - lambda_init)
    o_ref[...] = y.astype(o_ref.dtype)


def _diff_attn(qk, v, lam_vecs, g_subln, batch, seq, heads, dk, lambda_init, tq):
    dv = 2 * dk
    nq = seq // tq
    kernel = functools.partial(_diff_attn_kernel, tq=tq, tk=tq, dk=dk, lambda_init=lambda_init)
    return pl.pallas_call(
        kernel,
        grid=(batch, heads, nq),
        in_specs=[pl.BlockSpec((tq, dv), lambda b, h, i: (b * nq + i, h)),
                  pl.BlockSpec((seq, dv), lambda b, h, i: (b, heads + h)),
                  pl.BlockSpec((seq, dv), lambda b, h, i: (b, h)),
                  pl.BlockSpec((4, dk), lambda b, h, i: (0, 0)),
                  pl.BlockSpec((1, dv), lambda b, h, i: (0, 0))],
        out_specs=pl.BlockSpec((tq, dv), lambda b, h, i: (b * nq + i, h)),
        out_shape=jax.ShapeDtypeStruct((batch * seq, heads * dv), BF16),
        scratch_shapes=[pltpu.VMEM((2, tq, 1), F32), pltpu.VMEM((2, tq, 1), F32),
                        pltpu.VMEM((2, tq, dv), F32)],
        compiler_params=_params("parallel", "parallel", "arbitrary"),
        name="diff_attn",
    )(qk, qk, v, lam_vecs, g_subln.reshape(1, dv))


def _pool_kernel(up_ref, prev_ref, wp_ref, ps_ref, o_ref, ext_scr, *, tm, nseq, halo, gdim):
    i = pl.program_id(0)
    ti = i % nseq
    ext_scr[halo:halo + tm, :] = up_ref[...]
    ext_scr[0:halo, :] = jnp.where(ti == 0, 0.0, prev_ref[...])
    t1 = (ti * tm + 1 + lax.broadcasted_iota(jnp.int32, (tm, 1), 0)).astype(F32)
    for g, win in enumerate(POOL_WINDOWS):
        sl = slice(g * gdim, (g + 1) * gdim)
        cur = ext_scr[halo:halo + tm, sl]
        tot = cur
        for d in range(1, win):
            tot = tot + ext_scr[halo - d:halo - d + tm, sl]
        y = tot / jnp.minimum(t1, float(win)) - cur
        z = jnp.dot(y.astype(BF16), wp_ref[g], preferred_element_type=F32) * ps_ref[g:g + 1, :]
        o_ref[:, sl] = z.astype(o_ref.dtype)


def _pool_mixer(up, w_pool, pool_scale, seq, tm):
    m, width = up.shape
    groups, gdim, _ = w_pool.shape
    halo = 2 * SUBLANES
    assert max(POOL_WINDOWS) <= halo
    nseq = seq // tm
    per = tm // halo
    kernel = functools.partial(_pool_kernel, tm=tm, nseq=nseq, halo=halo, gdim=gdim)
    return pl.pallas_call(
        kernel,
        grid=(m // tm,),
        in_specs=[pl.BlockSpec((tm, width), lambda i: (i, 0)),
                  pl.BlockSpec((halo, width), lambda i: (jnp.maximum(i * per - 1, 0), 0)),
                  pl.BlockSpec((groups, gdim, gdim), lambda i: (0, 0, 0)),
                  pl.BlockSpec((groups, gdim), lambda i: (0, 0))],
        out_specs=pl.BlockSpec((tm, width), lambda i: (i, 0)),
        out_shape=jax.ShapeDtypeStruct((m, width), BF16),
        scratch_shapes=[pltpu.VMEM((tm + halo, width), F32)],
        compiler_params=_params("parallel"),
        name="pool_mixer",
    )(up, up, w_pool, pool_scale)


def _mem_attn_kernel(q_ref, k_ref, v_ref, o_ref, *, heads, hd):
    scale = hd ** -0.5
    for h in range(heads):
        sl = slice(h * hd, (h + 1) * hd)
        s = lax.dot_general(q_ref[:, sl], k_ref[:, sl], (((1,), (1,)), ((), ())),
                            preferred_element_type=F32) * scale
        p = jnp.exp(s - jnp.max(s, axis=-1, keepdims=True))
        l = jnp.sum(p, axis=-1, keepdims=True)
        o = jnp.dot(p.astype(BF16), v_ref[:, sl], preferred_element_type=F32) / l
        o_ref[:, sl] = o.astype(o_ref.dtype)


def _mem_attn(qm, km, vm, batch, seq, mem_len, heads, tq):
    m, width = qm.shape
    nq = seq // tq
    kernel = functools.partial(_mem_attn_kernel, heads=heads, hd=width // heads)
    return pl.pallas_call(
        kernel,
        grid=(batch, nq),
        in_specs=[pl.BlockSpec((tq, width), lambda b, i: (b * nq + i, 0)),
                  pl.BlockSpec((mem_len, width), lambda b, i: (b, 0)),
                  pl.BlockSpec((mem_len, width), lambda b, i: (b, 0))],
        out_specs=pl.BlockSpec((tq, width), lambda b, i: (b * nq + i, 0)),
        out_shape=jax.ShapeDtypeStruct((m, width), BF16),
        compiler_params=_params("parallel", "parallel"),
        name="mem_attn",
    )(qm, km, vm)


def _merge_kernel(ya_ref, yb_ref, yc_ref, wa_ref, wb_ref, wc_ref, ga_ref, gb_ref, gc_ref, o_ref):
    out = jax.nn.sigmoid(ga_ref[...]) * jnp.dot(ya_ref[...], wa_ref[...], preferred_element_type=F32)
    out = out + jax.nn.sigmoid(gb_ref[...]) * jnp.dot(yb_ref[...], wb_ref[...],
                                                      preferred_element_type=F32)
    out = out + jax.nn.sigmoid(gc_ref[...]) * jnp.dot(yc_ref[...], wc_ref[...],
                                                      preferred_element_type=F32)
    o_ref[...] = out.astype(o_ref.dtype)


def _merge(ya, yb, yc, w_a, w_b, w_c, gl, tm, tn):
    m = ya.shape[0]
    d = w_a.shape[1]
    nj = d // tn
    row = lambda a: pl.BlockSpec((tm, a.shape[1]), lambda i, j: (i, 0))
    col = lambda w: pl.BlockSpec((w.shape[0], tn), lambda i, j: (0, j))
    gate = lambda b: pl.BlockSpec((tm, tn), lambda i, j: (i, b * nj + j))
    return pl.pallas_call(
        _merge_kernel,
        grid=(m // tm, nj),
        in_specs=[row(ya), row(yb), row(yc), col(w_a), col(w_b), col(w_c), gate(0), gate(1), gate(2)],
        out_specs=pl.BlockSpec((tm, tn), lambda i, j: (i, j)),
        out_shape=jax.ShapeDtypeStruct((m, d), BF16),
        compiler_params=_params("parallel", "parallel"),
        name="merge",
    )(ya, yb, yc, w_a, w_b, w_c, gl, gl, gl)


def _proj_residual_kernel(a_ref, w_ref, r_ref, o_ref):
    o_ref[...] = r_ref[...] + jnp.dot(a_ref[...], w_ref[...], preferred_element_type=F32)


def _proj_residual(a, w, res, tm, tn):
    m, k = a.shape
    n = w.shape[1]
    return pl.pallas_call(
        _proj_residual_kernel,
        grid=(m // tm, n // tn),
        in_specs=[pl.BlockSpec((tm, k), lambda i, j: (i, 0)),
                  pl.BlockSpec((k, tn), lambda i, j: (0, j)),
                  pl.BlockSpec((tm, tn), lambda i, j: (i, j))],
        out_specs=pl.BlockSpec((tm, tn), lambda i, j: (i, j)),
        out_shape=jax.ShapeDtypeStruct((m, n), F32),
        compiler_params=_params("parallel", "parallel"),
        name="proj_residual",
    )(a, w, res)


def _ffn_up_kernel(h_ref, wg_ref, wv_ref, cwg_ref, cwv_ref, cbg_ref, cbv_ref, o_ref, yg_scr, yv_scr,
                   *, tm, nseq):
    i = pl.program_id(1)
    pad = SUBLANES

    def conv(y_scr, w_ref, cw_ref, cb_ref):
        @pl.when(i % nseq == 0)
        def _():
            y_scr[0:pad, :] = jnp.zeros((pad, y_scr.shape[1]), F32)

        y_scr[pad:pad + tm, :] = jnp.dot(h_ref[...], w_ref[...], preferred_element_type=F32)
        u = cb_ref[...] + y_scr[pad:pad + tm, :] * cw_ref[CONV_W - 1:CONV_W, :]
        for d in range(1, CONV_W):
            u = u + y_scr[pad - d:pad - d + tm, :] * cw_ref[CONV_W - 1 - d:CONV_W - d, :]
        y_scr[0:pad, :] = y_scr[tm:tm + pad, :]
        return u

    gate = conv(yg_scr, wg_ref, cwg_ref, cbg_ref)
    val = conv(yv_scr, wv_ref, cwv_ref, cbv_ref)
    o_ref[...] = (gate * jax.nn.sigmoid(gate) * val).astype(o_ref.dtype)


def _ffn_up(h, w_up, conv_w, conv_b, seq, tm, tn):
    m, k = h.shape
    ff = w_up.shape[1] // 2
    nj = ff // tn
    nseq = seq // tm
    kernel = functools.partial(_ffn_up_kernel, tm=tm, nseq=nseq)
    wspec = lambda half: pl.BlockSpec((k, tn), lambda j, i: (0, half * nj + j))
    cwspec = lambda half: pl.BlockSpec((CONV_W, tn), lambda j, i: (0, half * nj + j))
    cbspec = lambda half: pl.BlockSpec((1, tn), lambda j, i: (0, half * nj + j))
    return pl.pallas_call(
        kernel,
        grid=(nj, m // tm),
        in_specs=[pl.BlockSpec((tm, k), lambda j, i: (i, 0)),
                  wspec(0), wspec(1), cwspec(0), cwspec(1), cbspec(0), cbspec(1)],
        out_specs=pl.BlockSpec((tm, tn), lambda j, i: (i, j)),
        out_shape=jax.ShapeDtypeStruct((m, ff), BF16),
        scratch_shapes=[pltpu.VMEM((tm + SUBLANES, tn), F32), pltpu.VMEM((tm + SUBLANES, tn), F32)],
        compiler_params=_params("arbitrary", "arbitrary"),
        name="ffn_up",
    )(h, w_up, w_up, conv_w, conv_w, conv_b.reshape(1, -1), conv_b.reshape(1, -1))


def _ffn_down_kernel(a_ref, w_ref, r_ref, o_ref, acc_scr):
    kk = pl.program_id(2)

    @pl.when(kk == 0)
    def _():
        acc_scr[...] = r_ref[...]

    acc_scr[...] += jnp.dot(a_ref[...], w_ref[...], preferred_element_type=F32)

    @pl.when(kk == pl.num_programs(2) - 1)
    def _():
        o_ref[...] = acc_scr[...]


def _ffn_down(a, w, res, tm, tn, tk):
    m, k = a.shape
    n = w.shape[1]
    return pl.pallas_call(
        _ffn_down_kernel,
        grid=(m // tm, n // tn, k // tk),
        in_specs=[pl.BlockSpec((tm, tk), lambda i, j, kk: (i, kk)),
                  pl.BlockSpec((tk, tn), lambda i, j, kk: (kk, j)),
                  pl.BlockSpec((tm, tn), lambda i, j, kk: (i, j))],
        out_specs=pl.BlockSpec((tm, tn), lambda i, j, kk: (i, j)),
        out_shape=jax.ShapeDtypeStruct((m, n), F32),
        scratch_shapes=[pltpu.VMEM((tm, tn), F32)],
        compiler_params=_params("parallel", "parallel", "arbitrary"),
        name="ffn_down",
    )(a, w, res)


def _rope_tables(seq, rope_dim):
    half = rope_dim // 2
    pos = jnp.arange(seq, dtype=F32)
    inv = ROPE_THETA ** (-jnp.arange(0, rope_dim, 2, dtype=F32) / rope_dim)
    ang = pos[:, None] * inv[None, :]
    cos, sin = jnp.cos(ang), jnp.sin(ang)
    ones = jnp.ones((seq, LANES - rope_dim), F32)
    zeros = jnp.zeros((seq, LANES - rope_dim), F32)
    zh = jnp.zeros((seq, half), F32)
    cos_t = jnp.concatenate([cos, cos, ones], axis=1)
    sa_t = jnp.concatenate([-sin, zh, zeros], axis=1)
    sb_t = jnp.concatenate([zh, sin, zeros], axis=1)
    return cos_t, sa_t, sb_t


def _pad_halves(a, ff, ff_pad):
    pad = [(0, 0)] * (a.ndim - 1) + [(0, ff_pad - ff)]
    return jnp.concatenate([jnp.pad(a[..., :ff], pad), jnp.pad(a[..., ff:], pad)], axis=-1)


def kernel(x, mem, g_attn_norm, w_in, g_qa, g_ka, lam_q1, lam_k1, lam_q2, lam_k2, g_subln, w_pool,
           pool_scale, g_mem, w_mkv, g_qm, g_km, w_a, w_b, w_c, w_o, g_ffn_norm, w_up, conv_w, conv_b,
           w_down):
    batch, seq, d_model = x.shape
    mem_len = mem.shape[1]
    depth = w_in.shape[0]
    dk = g_qa.shape[1]
    dv = g_subln.shape[1]
    heads = w_a.shape[1] // dv
    qk_width = heads * 2 * dk
    pool_width = w_b.shape[1]
    mem_width = w_c.shape[1]
    mem_hd = g_qm.shape[1]
    rope_dim = dk // 4
    d_ff = w_down.shape[1]
    ff_pad = -(-d_ff // 1024) * 1024
    assert dk == LANES and dv == 2 * dk

    m = batch * seq
    xf = x.reshape(m, d_model)
    cos_t, sa_t, sb_t = _rope_tables(seq, rope_dim)
    c_v = 2 * qk_width
    c_up = c_v + heads * dv
    c_qm = c_up + pool_width
    c_gl = c_qm + mem_width

    for l in range(depth):
        lambda_init = 0.8 - 0.6 * math.exp(-0.3 * l)
        w_in_b = w_in[l].astype(BF16)
        g_qk = jnp.concatenate([jnp.tile(g_qa[l], qk_width // dk), jnp.tile(g_ka[l], qk_width // dk)])
        lam_vecs = jnp.stack([lam_q1[l], lam_k1[l], lam_q2[l], lam_k2[l]]).astype(F32)

        h = _rmsnorm(xf, g_attn_norm[l], 256)
        qk = _proj_qk(h, w_in_b, g_qk, cos_t, sa_t, sb_t, seq, rope_dim, 1024, 1024)
        v = _proj_plain(h, w_in_b, c_v, heads * dv, BF16, 1024, 1024, "proj_v")
        up = _proj_plain(h, w_in_b, c_up, pool_width, F32, 1024, 1024, "proj_up")
        qm = _proj_norm(h, w_in_b, g_qm[l], c_qm, mem_width, 1024, 1024, "proj_qm")
        gl = _proj_plain(h, w_in_b, c_gl, 3 * d_model, F32, 1024, 1024, "proj_gates")

        ya = _diff_attn(qk, v, lam_vecs, g_subln[l], batch, seq, heads, dk, lambda_init, 512)
        yb = _pool_mixer(up, w_pool[l].astype(BF16), pool_scale[l], seq, 512)

        mem_n = _rmsnorm(mem.reshape(batch * mem_len, d_model), g_mem[l], 256)
        w_mkv_b = w_mkv[l].astype(BF16)
        km = _proj_norm(mem_n, w_mkv_b, g_km[l], 0, mem_width, batch * mem_len, 1024, "proj_km")
        vm = _proj_plain(mem_n, w_mkv_b, mem_width, mem_width, BF16, batch * mem_len, 1024, "proj_vm")
        yc = _mem_attn(qm, km, vm, batch, seq, mem_len, mem_width // mem_hd, 512)

        merged = _merge(ya, yb, yc, w_a[l].astype(BF16), w_b[l].astype(BF16), w_c[l].astype(BF16),
                        gl, 1024, 512)
        xf = _proj_residual(merged, w_o[l].astype(BF16), xf, 1024, 512)

        h2 = _rmsnorm(xf, g_ffn_norm[l], 256)
        w_up_b = _pad_halves(w_up[l], d_ff, ff_pad).astype(BF16)
        act = _ffn_up(h2, w_up_b, _pad_halves(conv_w[l], d_ff, ff_pad),
                      _pad_halves(conv_b[l], d_ff, ff_pad), seq, 1024, 512)
        w_down_b = jnp.pad(w_down[l], ((0, ff_pad - d_ff), (0, 0))).astype(BF16)
        xf = _ffn_down(act, w_down_b, xf, 1024, 1024, ff_pad // 4)

    return xf.reshape(batch, seq, d_model)
```

```python
import functools
import math

import jax
import jax.numpy as jnp
from jax import lax
from jax.experimental import pallas as pl
from jax.experimental.pallas import tpu as pltpu

F32 = jnp.float32
BF16 = jnp.bfloat16

EPS = 1e-6
ROPE_THETA = 500000.0
CHUNK = 64
POOL_WINDOWS = (2, 4, 8, 16)
CONV_W = 3

LANES = 128
SUBLANES = 8
VMEM_LIMIT_BYTES = 56 * 1024 * 1024
ROW_CHUNK = 256
LOG2E = 1.4426950408889634


def _params(*sem):
    return pltpu.CompilerParams(dimension_semantics=sem, vmem_limit_bytes=VMEM_LIMIT_BYTES)


def _group_rmsnorm(x, g):
    ms = jnp.mean(x * x, axis=-1, keepdims=True)
    return x * lax.rsqrt(ms + EPS) * g


def _rmsnorm_kernel(x_ref, g_ref, o_ref):
    o_ref[...] = _group_rmsnorm(x_ref[...], g_ref[...]).astype(o_ref.dtype)


def _rmsnorm(x, g, tm):
    m, d = x.shape
    return pl.pallas_call(
        _rmsnorm_kernel,
        grid=(m // tm,),
        in_specs=[pl.BlockSpec((tm, d), lambda i: (i, 0)), pl.BlockSpec((1, d), lambda i: (0, 0))],
        out_specs=pl.BlockSpec((tm, d), lambda i: (i, 0)),
        out_shape=jax.ShapeDtypeStruct((m, d), BF16),
        compiler_params=_params("parallel"),
        name="rmsnorm",
    )(x, g.reshape(1, d))


def _proj_plain_kernel(x_ref, w_ref, o_ref):
    o_ref[...] = jnp.dot(x_ref[...], w_ref[...], preferred_element_type=F32).astype(o_ref.dtype)


def _proj_plain(x, w, col0, ncols, out_dtype, tm, tn, name):
    m, k = x.shape
    j0 = col0 // tn
    return pl.pallas_call(
        _proj_plain_kernel,
        grid=(m // tm, ncols // tn),
        in_specs=[pl.BlockSpec((tm, k), lambda i, j: (i, 0)),
                  pl.BlockSpec((k, tn), lambda i, j: (0, j0 + j))],
        out_specs=pl.BlockSpec((tm, tn), lambda i, j: (i, j)),
        out_shape=jax.ShapeDtypeStruct((m, ncols), out_dtype),
        compiler_params=_params("parallel", "parallel"),
        name=name,
    )(x, w)


def _proj_norm_kernel(x_ref, w_ref, g_ref, o_ref, *, gdim):
    acc = jnp.dot(x_ref[...], w_ref[...], preferred_element_type=F32)
    for c in range(acc.shape[1] // gdim):
        sl = slice(c * gdim, (c + 1) * gdim)
        o_ref[:, sl] = _group_rmsnorm(acc[:, sl], g_ref[...]).astype(o_ref.dtype)


def _proj_norm(x, w, g, col0, ncols, tm, tn, name):
    m, k = x.shape
    gdim = g.shape[0]
    j0 = col0 // tn
    return pl.pallas_call(
        functools.partial(_proj_norm_kernel, gdim=gdim),
        grid=(m // tm, ncols // tn),
        in_specs=[pl.BlockSpec((tm, k), lambda i, j: (i, 0)),
                  pl.BlockSpec((k, tn), lambda i, j: (0, j0 + j)),
                  pl.BlockSpec((1, gdim), lambda i, j: (0, 0))],
        out_specs=pl.BlockSpec((tm, tn), lambda i, j: (i, j)),
        out_shape=jax.ShapeDtypeStruct((m, ncols), BF16),
        compiler_params=_params("parallel", "parallel"),
        name=name,
    )(x, w, g.reshape(1, gdim))


def _proj_qk_kernel(x_ref, w_ref, g_ref, cos_ref, sa_ref, sb_ref, o_ref, *, rope_dim):
    half = rope_dim // 2
    for r in range(x_ref.shape[0] // ROW_CHUNK):
        rows = slice(r * ROW_CHUNK, (r + 1) * ROW_CHUNK)
        acc = jnp.dot(x_ref[rows, :], w_ref[...], preferred_element_type=F32)
        for c in range(acc.shape[1] // LANES):
            sl = slice(c * LANES, (c + 1) * LANES)
            y = _group_rmsnorm(acc[:, sl], g_ref[:, sl])
            y = (y * cos_ref[rows, :] + pltpu.roll(y, LANES - half, 1) * sa_ref[rows, :]
                 + pltpu.roll(y, half, 1) * sb_ref[rows, :])
            o_ref[rows, sl] = y.astype(o_ref.dtype)


def _proj_qk(x, w, g_cols, cos_t, sa_t, sb_t, seq, rope_dim, tm, tn):
    m, k = x.shape
    ncols = g_cols.shape[0]
    nseq = seq // tm
    tab = pl.BlockSpec((tm, LANES), lambda i, j: (i % nseq, 0))
    return pl.pallas_call(
        functools.partial(_proj_qk_kernel, rope_dim=rope_dim),
        grid=(m // tm, ncols // tn),
        in_specs=[pl.BlockSpec((tm, k), lambda i, j: (i, 0)),
                  pl.BlockSpec((k, tn), lambda i, j: (0, j)),
                  pl.BlockSpec((1, tn), lambda i, j: (0, j)),
                  tab, tab, tab],
        out_specs=pl.BlockSpec((tm, tn), lambda i, j: (i, j)),
        out_shape=jax.ShapeDtypeStruct((m, ncols), BF16),
        compiler_params=_params("parallel", "parallel"),
        name="proj_qk",
    )(x, w, g_cols.reshape(1, ncols), cos_t, sa_t, sb_t)


def _proj_t_kernel(x_ref, wt_ref, o_ref):
    res = lax.dot_general(wt_ref[...], x_ref[...], (((1,), (1,)), ((), ())),
                          preferred_element_type=F32).astype(o_ref.dtype)
    tk = o_ref.shape[2]
    for r in range(o_ref.shape[0]):
        o_ref[r] = res[:, r * tk:(r + 1) * tk]


def _proj_t(x, wt, tk, tm, tn):
    m, k = x.shape
    n = wt.shape[0]
    return pl.pallas_call(
        _proj_t_kernel,
        grid=(m // tm, n // tn),
        in_specs=[pl.BlockSpec((tm, k), lambda i, j: (i, 0)),
                  pl.BlockSpec((tn, k), lambda i, j: (j, 0))],
        out_specs=pl.BlockSpec((tm // tk, tn, tk), lambda i, j: (i, j, 0)),
        out_shape=jax.ShapeDtypeStruct((m // tk, n, tk), BF16),
        compiler_params=_params("parallel", "parallel"),
        name="proj_vt",
    )(x, wt)


def _diff_attn_kernel(q_ref, k_ref, vt_ref, lam_ref, gs_ref, o_ref,
                      s_scr, p_scr, a_scr, m_scr, l_scr, acc_scr, *, tq, tk, dk, lambda_init):
    i = pl.program_id(2)
    c0 = dk ** -0.5 * LOG2E
    neg = -1e30
    q = q_ref[...]
    m_scr[...] = jnp.full(m_scr.shape, neg, F32)
    l_scr[...] = jnp.zeros(l_scr.shape, F32)
    acc_scr[...] = jnp.zeros(acc_scr.shape, F32)

    def scores(blk, slot):
        k = k_ref[pl.ds(pl.multiple_of(blk * tk, tk), tk), :]
        for c in range(2):
            sl = slice(c * dk, (c + 1) * dk)
            s_scr[slot, c] = lax.dot_general(k[:, sl], q[:, sl], (((1,), (1,)), ((), ())),
                                             preferred_element_type=F32)

    def softmax(slot, diag=None):
        if diag is not None:
            key_chunk = (diag * tk + lax.broadcasted_iota(jnp.int32, (tk, tq), 0)) // CHUNK
            q_chunk = lax.broadcasted_iota(jnp.int32, (tk, tq), 1) // CHUNK
            allowed = key_chunk <= q_chunk
        for c in range(2):
            s = s_scr[slot, c]
            if diag is not None:
                s = jnp.where(allowed, s, neg)
            m_prev = m_scr[c]
            m_new = jnp.maximum(m_prev, jnp.max(s, axis=0, keepdims=True))
            alpha = jnp.exp2((m_prev - m_new) * c0)
            p = jnp.exp2((s - m_new) * c0)
            l_scr[c] = alpha * l_scr[c] + jnp.sum(p, axis=0, keepdims=True)
            p_scr[slot, c] = p.astype(BF16)
            a_scr[slot, c] = alpha
            m_scr[c] = m_new

    def values(slot, blk):
        vt = vt_ref[blk]
        for c in range(2):
            acc_scr[c] = a_scr[slot, c] * acc_scr[c] + jnp.dot(vt, p_scr[slot, c],
                                                               preferred_element_type=F32)

    scores(2 * i, 0)
    scores(2 * i + 1, 1)
    softmax(0, diag=0)
    scores(0, 0)
    softmax(1, diag=1)
    values(0, 2 * i)

    def pair(j, last):
        scores(2 * j - 1, 1)
        softmax(0)
        values(1, jnp.where(j == 1, 2 * i + 1, 2 * j - 3))
        if not last:
            scores(2 * j, 0)
        softmax(1)
        values(0, 2 * j - 2)

    def body(j, carry):
        pair(j, False)
        return carry

    lax.fori_loop(1, i, body, 0)

    @pl.when(i >= 1)
    def _():
        pair(i, True)

    values(1, jnp.where(i == 0, 1, 2 * i - 1))

    lam_v = lam_ref[...]
    lam = (jnp.exp(jnp.sum(lam_v[0:1] * lam_v[1:2], axis=-1, keepdims=True))
           - jnp.exp(jnp.sum(lam_v[2:3] * lam_v[3:4], axis=-1, keepdims=True)) + lambda_init)
    o = acc_scr[0] / l_scr[0] - lam * (acc_scr[1] / l_scr[1])
    ms = jnp.mean(o * o, axis=0, keepdims=True)
    y = o * lax.rsqrt(ms + EPS) * gs_ref[...] * (1.0 - lambda_init)
    o_ref[...] = y.T.astype(o_ref.dtype)


def _diff_attn(qk, vt, lam_vecs, g_subln, batch, seq, heads, dk, lambda_init, tq):
    dv = 2 * dk
    tk = tq // 2
    nq = seq // tq
    kernel = functools.partial(_diff_attn_kernel, tq=tq, tk=tk, dk=dk, lambda_init=lambda_init)
    return pl.pallas_call(
        kernel,
        grid=(batch, heads, nq),
        in_specs=[pl.BlockSpec((tq, dv), lambda b, h, i: (b * nq + i, h)),
                  pl.BlockSpec((seq, dv), lambda b, h, i: (b, heads + h)),
                  pl.BlockSpec((seq // tk, dv, tk), lambda b, h, i: (b, h, 0)),
                  pl.BlockSpec((4, dk), lambda b, h, i: (0, 0)),
                  pl.BlockSpec((dv, 1), lambda b, h, i: (0, 0))],
        out_specs=pl.BlockSpec((tq, dv), lambda b, h, i: (b * nq + i, h)),
        out_shape=jax.ShapeDtypeStruct((batch * seq, heads * dv), BF16),
        scratch_shapes=[pltpu.VMEM((2, 2, tk, tq), F32), pltpu.VMEM((2, 2, tk, tq), BF16),
                        pltpu.VMEM((2, 2, 1, tq), F32), pltpu.VMEM((2, 1, tq), F32),
                        pltpu.VMEM((2, 1, tq), F32), pltpu.VMEM((2, dv, tq), F32)],
        compiler_params=_params("parallel", "parallel", "arbitrary"),
        name="diff_attn",
    )(qk, qk, vt, lam_vecs, g_subln.reshape(dv, 1))


def _pool_kernel(up_ref, prev_ref, wp_ref, ps_ref, o_ref, ext_scr, *, tm, nseq, halo, gdim):
    i = pl.program_id(0)
    ti = i % nseq
    ext_scr[halo:halo + tm, :] = up_ref[...]
    ext_scr[0:halo, :] = jnp.where(ti == 0, 0.0, prev_ref[...])
    t1 = (ti * tm + 1 + lax.broadcasted_iota(jnp.int32, (tm, 1), 0)).astype(F32)
    for g, win in enumerate(POOL_WINDOWS):
        sl = slice(g * gdim, (g + 1) * gdim)
        cur = ext_scr[halo:halo + tm, sl]
        tot = cur
        for d in range(1, win):
            tot = tot + ext_scr[halo - d:halo - d + tm, sl]
        y = tot / jnp.minimum(t1, float(win)) - cur
        z = jnp.dot(y.astype(BF16), wp_ref[g], preferred_element_type=F32) * ps_ref[g:g + 1, :]
        o_ref[:, sl] = z.astype(o_ref.dtype)


def _pool_mixer(up, w_pool, pool_scale, seq, tm):
    m, width = up.shape
    groups, gdim, _ = w_pool.shape
    halo = 2 * SUBLANES
    assert max(POOL_WINDOWS) <= halo
    nseq = seq // tm
    per = tm // halo
    kernel = functools.partial(_pool_kernel, tm=tm, nseq=nseq, halo=halo, gdim=gdim)
    return pl.pallas_call(
        kernel,
        grid=(m // tm,),
        in_specs=[pl.BlockSpec((tm, width), lambda i: (i, 0)),
                  pl.BlockSpec((halo, width), lambda i: (jnp.maximum(i * per - 1, 0), 0)),
                  pl.BlockSpec((groups, gdim, gdim), lambda i: (0, 0, 0)),
                  pl.BlockSpec((groups, gdim), lambda i: (0, 0))],
        out_specs=pl.BlockSpec((tm, width), lambda i: (i, 0)),
        out_shape=jax.ShapeDtypeStruct((m, width), BF16),
        scratch_shapes=[pltpu.VMEM((tm + halo, width), F32)],
        compiler_params=_params("parallel"),
        name="pool_mixer",
    )(up, up, w_pool, pool_scale)


def _mem_attn_kernel(q_ref, k_ref, v_ref, o_ref, *, heads, hd):
    scale = hd ** -0.5
    for h in range(heads):
        sl = slice(h * hd, (h + 1) * hd)
        s = lax.dot_general(q_ref[:, sl], k_ref[:, sl], (((1,), (1,)), ((), ())),
                            preferred_element_type=F32) * scale
        p = jnp.exp(s - jnp.max(s, axis=-1, keepdims=True))
        l = jnp.sum(p, axis=-1, keepdims=True)
        o = jnp.dot(p.astype(BF16), v_ref[:, sl], preferred_element_type=F32) / l
        o_ref[:, sl] = o.astype(o_ref.dtype)


def _mem_attn(qm, km, vm, batch, seq, mem_len, heads, tq):
    m, width = qm.shape
    nq = seq // tq
    kernel = functools.partial(_mem_attn_kernel, heads=heads, hd=width // heads)
    return pl.pallas_call(
        kernel,
        grid=(batch, nq),
        in_specs=[pl.BlockSpec((tq, width), lambda b, i: (b * nq + i, 0)),
                  pl.BlockSpec((mem_len, width), lambda b, i: (b, 0)),
                  pl.BlockSpec((mem_len, width), lambda b, i: (b, 0))],
        out_specs=pl.BlockSpec((tq, width), lambda b, i: (b * nq + i, 0)),
        out_shape=jax.ShapeDtypeStruct((m, width), BF16),
        compiler_params=_params("parallel", "parallel"),
        name="mem_attn",
    )(qm, km, vm)


def _merge_kernel(ya_ref, yb_ref, yc_ref, wa_ref, wb_ref, wc_ref, ga_ref, gb_ref, gc_ref, o_ref):
    out = jax.nn.sigmoid(ga_ref[...]) * jnp.dot(ya_ref[...], wa_ref[...], preferred_element_type=F32)
    out = out + jax.nn.sigmoid(gb_ref[...]) * jnp.dot(yb_ref[...], wb_ref[...],
                                                      preferred_element_type=F32)
    out = out + jax.nn.sigmoid(gc_ref[...]) * jnp.dot(yc_ref[...], wc_ref[...],
                                                      preferred_element_type=F32)
    o_ref[...] = out.astype(o_ref.dtype)


def _merge(ya, yb, yc, w_a, w_b, w_c, gl, tm, tn):
    m = ya.shape[0]
    d = w_a.shape[1]
    nj = d // tn
    row = lambda a: pl.BlockSpec((tm, a.shape[1]), lambda i, j: (i, 0))
    col = lambda w: pl.BlockSpec((w.shape[0], tn), lambda i, j: (0, j))
    gate = lambda b: pl.BlockSpec((tm, tn), lambda i, j: (i, b * nj + j))
    return pl.pallas_call(
        _merge_kernel,
        grid=(m // tm, nj),
        in_specs=[row(ya), row(yb), row(yc), col(w_a), col(w_b), col(w_c), gate(0), gate(1), gate(2)],
        out_specs=pl.BlockSpec((tm, tn), lambda i, j: (i, j)),
        out_shape=jax.ShapeDtypeStruct((m, d), BF16),
        compiler_params=_params("parallel", "parallel"),
        name="merge",
    )(ya, yb, yc, w_a, w_b, w_c, gl, gl, gl)


def _proj_residual_kernel(a_ref, w_ref, r_ref, o_ref):
    o_ref[...] = r_ref[...] + jnp.dot(a_ref[...], w_ref[...], preferred_element_type=F32)


def _proj_residual(a, w, res, tm, tn):
    m, k = a.shape
    n = w.shape[1]
    return pl.pallas_call(
        _proj_residual_kernel,
        grid=(m // tm, n // tn),
        in_specs=[pl.BlockSpec((tm, k), lambda i, j: (i, 0)),
                  pl.BlockSpec((k, tn), lambda i, j: (0, j)),
                  pl.BlockSpec((tm, tn), lambda i, j: (i, j))],
        out_specs=pl.BlockSpec((tm, tn), lambda i, j: (i, j)),
        out_shape=jax.ShapeDtypeStruct((m, n), F32),
        compiler_params=_params("parallel", "parallel"),
        name="proj_residual",
    )(a, w, res)


def _ffn_up_kernel(h_ref, wg_ref, wv_ref, cwg_ref, cwv_ref, cbg_ref, cbv_ref, o_ref, yg_scr, yv_scr,
                   *, tm, nseq):
    i = pl.program_id(1)
    pad = SUBLANES

    @pl.when(i % nseq == 0)
    def _():
        yg_scr[0:pad, :] = jnp.zeros((pad, yg_scr.shape[1]), F32)
        yv_scr[0:pad, :] = jnp.zeros((pad, yv_scr.shape[1]), F32)

    def conv(r, y_scr, w_ref, cw_ref, cb_ref):
        lo = pad + r * ROW_CHUNK
        y_scr[lo:lo + ROW_CHUNK, :] = jnp.dot(h_ref[r * ROW_CHUNK:(r + 1) * ROW_CHUNK, :], w_ref[...],
                                              preferred_element_type=F32)
        u = cb_ref[...] + y_scr[lo:lo + ROW_CHUNK, :] * cw_ref[CONV_W - 1:CONV_W, :]
        for d in range(1, CONV_W):
            u = u + y_scr[lo - d:lo - d + ROW_CHUNK, :] * cw_ref[CONV_W - 1 - d:CONV_W - d, :]
        return u

    for r in range(tm // ROW_CHUNK):
        gate = conv(r, yg_scr, wg_ref, cwg_ref, cbg_ref)
        val = conv(r, yv_scr, wv_ref, cwv_ref, cbv_ref)
        o_ref[r * ROW_CHUNK:(r + 1) * ROW_CHUNK, :] = (gate * jax.nn.sigmoid(gate) * val).astype(o_ref.dtype)
    yg_scr[0:pad, :] = yg_scr[tm:tm + pad, :]
    yv_scr[0:pad, :] = yv_scr[tm:tm + pad, :]


def _ffn_up(h, w_up, conv_w, conv_b, seq, tm, tn):
    m, k = h.shape
    ff = w_up.shape[1] // 2
    nj = ff // tn
    nseq = seq // tm
    kernel = functools.partial(_ffn_up_kernel, tm=tm, nseq=nseq)
    wspec = lambda half: pl.BlockSpec((k, tn), lambda j, i: (0, half * nj + j))
    cwspec = lambda half: pl.BlockSpec((CONV_W, tn), lambda j, i: (0, half * nj + j))
    cbspec = lambda half: pl.BlockSpec((1, tn), lambda j, i: (0, half * nj + j))
    return pl.pallas_call(
        kernel,
        grid=(nj, m // tm),
        in_specs=[pl.BlockSpec((tm, k), lambda j, i: (i, 0)),
                  wspec(0), wspec(1), cwspec(0), cwspec(1), cbspec(0), cbspec(1)],
        out_specs=pl.BlockSpec((tm, tn), lambda j, i: (i, j)),
        out_shape=jax.ShapeDtypeStruct((m, ff), BF16),
        scratch_shapes=[pltpu.VMEM((tm + SUBLANES, tn), F32), pltpu.VMEM((tm + SUBLANES, tn), F32)],
        compiler_params=_params("arbitrary", "arbitrary"),
        name="ffn_up",
    )(h, w_up, w_up, conv_w, conv_w, conv_b.reshape(1, -1), conv_b.reshape(1, -1))


def _ffn_down_kernel(a_ref, w_ref, r_ref, o_ref, acc_scr):
    kk = pl.program_id(2)

    @pl.when(kk == 0)
    def _():
        acc_scr[...] = r_ref[...]

    acc_scr[...] += jnp.dot(a_ref[...], w_ref[...], preferred_element_type=F32)

    @pl.when(kk == pl.num_programs(2) - 1)
    def _():
        o_ref[...] = acc_scr[...]


def _ffn_down(a, w, res, tm, tn, tk):
    m, k = a.shape
    n = w.shape[1]
    return pl.pallas_call(
        _ffn_down_kernel,
        grid=(m // tm, n // tn, k // tk),
        in_specs=[pl.BlockSpec((tm, tk), lambda i, j, kk: (i, kk)),
                  pl.BlockSpec((tk, tn), lambda i, j, kk: (kk, j)),
                  pl.BlockSpec((tm, tn), lambda i, j, kk: (i, j))],
        out_specs=pl.BlockSpec((tm, tn), lambda i, j, kk: (i, j)),
        out_shape=jax.ShapeDtypeStruct((m, n), F32),
        scratch_shapes=[pltpu.VMEM((tm, tn), F32)],
        compiler_params=_params("parallel", "parallel", "arbitrary"),
        name="ffn_down",
    )(a, w, res)


def _rope_tables(seq, rope_dim):
    half = rope_dim // 2
    pos = jnp.arange(seq, dtype=F32)
    inv = ROPE_THETA ** (-jnp.arange(0, rope_dim, 2, dtype=F32) / rope_dim)
    ang = pos[:, None] * inv[None, :]
    cos, sin = jnp.cos(ang), jnp.sin(ang)
    ones = jnp.ones((seq, LANES - rope_dim), F32)
    zeros = jnp.zeros((seq, LANES - rope_dim), F32)
    zh = jnp.zeros((seq, half), F32)
    cos_t = jnp.concatenate([cos, cos, ones], axis=1)
    sa_t = jnp.concatenate([-sin, zh, zeros], axis=1)
    sb_t = jnp.concatenate([zh, sin, zeros], axis=1)
    return cos_t, sa_t, sb_t


def _pad_halves(a, ff, ff_pad):
    pad = [(0, 0)] * (a.ndim - 1) + [(0, ff_pad - ff)]
    return jnp.concatenate([jnp.pad(a[..., :ff], pad), jnp.pad(a[..., ff:], pad)], axis=-1)


def kernel(x, mem, g_attn_norm, w_in, g_qa, g_ka, lam_q1, lam_k1, lam_q2, lam_k2, g_subln, w_pool,
           pool_scale, g_mem, w_mkv, g_qm, g_km, w_a, w_b, w_c, w_o, g_ffn_norm, w_up, conv_w, conv_b,
           w_down):
    batch, seq, d_model = x.shape
    mem_len = mem.shape[1]
    depth = w_in.shape[0]
    dk = g_qa.shape[1]
    dv = g_subln.shape[1]
    heads = w_a.shape[1] // dv
    qk_width = heads * 2 * dk
    pool_width = w_b.shape[1]
    mem_width = w_c.shape[1]
    mem_hd = g_qm.shape[1]
    rope_dim = dk // 4
    d_ff = w_down.shape[1]
    ff_pad = -(-d_ff // 1024) * 1024
    assert dk == LANES and dv == 2 * dk

    m = batch * seq
    xf = x.reshape(m, d_model)
    cos_t, sa_t, sb_t = _rope_tables(seq, rope_dim)
    c_v = 2 * qk_width
    c_up = c_v + heads * dv
    c_qm = c_up + pool_width
    c_gl = c_qm + mem_width

    for l in range(depth):
        lambda_init = 0.8 - 0.6 * math.exp(-0.3 * l)
        w_in_b = w_in[l].astype(BF16)
        g_qk = jnp.concatenate([jnp.tile(g_qa[l], qk_width // dk), jnp.tile(g_ka[l], qk_width // dk)])
        lam_vecs = jnp.stack([lam_q1[l], lam_k1[l], lam_q2[l], lam_k2[l]]).astype(F32)

        h = _rmsnorm(xf, g_attn_norm[l], 256)
        qk = _proj_qk(h, w_in_b, g_qk, cos_t, sa_t, sb_t, seq, rope_dim, 1024, 1024)
        attn_tile = 512
        w_vt = w_in[l][:, c_v:c_up].T.astype(BF16)
        vt = _proj_t(h, w_vt, attn_tile // 2, 1024, 1024)
        up = _proj_plain(h, w_in_b, c_up, pool_width, F32, 1024, 1024, "proj_up")
        qm = _proj_norm(h, w_in_b, g_qm[l], c_qm, mem_width, 1024, 1024, "proj_qm")
        gl = _proj_plain(h, w_in_b, c_gl, 3 * d_model, F32, 1024, 1024, "proj_gates")

        ya = _diff_attn(qk, vt, lam_vecs, g_subln[l], batch, seq, heads, dk, lambda_init, attn_tile)
        yb = _pool_mixer(up, w_pool[l].astype(BF16), pool_scale[l], seq, 512)

        mem_n = _rmsnorm(mem.reshape(batch * mem_len, d_model), g_mem[l], 256)
        w_mkv_b = w_mkv[l].astype(BF16)
        km = _proj_norm(mem_n, w_mkv_b, g_km[l], 0, mem_width, batch * mem_len, 1024, "proj_km")
        vm = _proj_plain(mem_n, w_mkv_b, mem_width, mem_width, BF16, batch * mem_len, 1024, "proj_vm")
        yc = _mem_attn(qm, km, vm, batch, seq, mem_len, mem_width // mem_hd, 512)

        merged = _merge(ya, yb, yc, w_a[l].astype(BF16), w_b[l].astype(BF16), w_c[l].astype(BF16),
                        gl, 1024, 512)
        xf = _proj_residual(merged, w_o[l].astype(BF16), xf, 1024, 512)

        h2 = _rmsnorm(xf, g_ffn_norm[l], 256)
        w_up_b = _pad_halves(w_up[l], d_ff, ff_pad).astype(BF16)
        act = _ffn_up(h2, w_up_b, _pad_halves(conv_w[l], d_ff, ff_pad),
                      _pad_halves(conv_b[l], d_ff, ff_pad), seq, 1024, 512)
        w_down_b = jnp.pad(w_down[l], ((0, ff_pad - d_ff), (0, 0))).astype(BF16)
        xf = _ffn_down(act, w_down_b, xf, 1024, 1024, ff_pad // 4)

    return xf.reshape(batch, seq, d_model)
```

```python
import functools
import math

import jax
import jax.numpy as jnp
from jax import lax
from jax.experimental import pallas as pl
from jax.experimental.pallas import tpu as pltpu

F32 = jnp.float32
BF16 = jnp.bfloat16

EPS = 1e-6
ROPE_THETA = 500000.0
CHUNK = 64
POOL_WINDOWS = (2, 4, 8, 16)
CONV_W = 3

LANES = 128
SUBLANES = 8
VMEM_LIMIT_BYTES = 56 * 1024 * 1024
ROW_CHUNK = 256
LOG2E = 1.4426950408889634


def _params(*sem):
    return pltpu.CompilerParams(dimension_semantics=sem, vmem_limit_bytes=VMEM_LIMIT_BYTES)


def _group_rmsnorm(x, g):
    ms = jnp.mean(x * x, axis=-1, keepdims=True)
    return x * lax.rsqrt(ms + EPS) * g


def _rmsnorm_kernel(x_ref, g_ref, o_ref):
    o_ref[...] = _group_rmsnorm(x_ref[...], g_ref[...]).astype(o_ref.dtype)


def _rmsnorm(x, g, tm):
    m, d = x.shape
    return pl.pallas_call(
        _rmsnorm_kernel,
        grid=(m // tm,),
        in_specs=[pl.BlockSpec((tm, d), lambda i: (i, 0)), pl.BlockSpec((1, d), lambda i: (0, 0))],
        out_specs=pl.BlockSpec((tm, d), lambda i: (i, 0)),
        out_shape=jax.ShapeDtypeStruct((m, d), BF16),
        compiler_params=_params("parallel"),
        name="rmsnorm",
    )(x, g.reshape(1, d))


def _proj_plain_kernel(x_ref, w_ref, o_ref):
    o_ref[...] = jnp.dot(x_ref[...], w_ref[...], preferred_element_type=F32).astype(o_ref.dtype)


def _proj_plain(x, w, col0, ncols, out_dtype, tm, tn, name):
    m, k = x.shape
    j0 = col0 // tn
    return pl.pallas_call(
        _proj_plain_kernel,
        grid=(m // tm, ncols // tn),
        in_specs=[pl.BlockSpec((tm, k), lambda i, j: (i, 0)),
                  pl.BlockSpec((k, tn), lambda i, j: (0, j0 + j))],
        out_specs=pl.BlockSpec((tm, tn), lambda i, j: (i, j)),
        out_shape=jax.ShapeDtypeStruct((m, ncols), out_dtype),
        compiler_params=_params("parallel", "parallel"),
        name=name,
    )(x, w)


def _proj_norm_kernel(x_ref, w_ref, g_ref, o_ref, *, gdim):
    acc = jnp.dot(x_ref[...], w_ref[...], preferred_element_type=F32)
    for c in range(acc.shape[1] // gdim):
        sl = slice(c * gdim, (c + 1) * gdim)
        o_ref[:, sl] = _group_rmsnorm(acc[:, sl], g_ref[...]).astype(o_ref.dtype)


def _proj_norm(x, w, g, col0, ncols, tm, tn, name):
    m, k = x.shape
    gdim = g.shape[0]
    j0 = col0 // tn
    return pl.pallas_call(
        functools.partial(_proj_norm_kernel, gdim=gdim),
        grid=(m // tm, ncols // tn),
        in_specs=[pl.BlockSpec((tm, k), lambda i, j: (i, 0)),
                  pl.BlockSpec((k, tn), lambda i, j: (0, j0 + j)),
                  pl.BlockSpec((1, gdim), lambda i, j: (0, 0))],
        out_specs=pl.BlockSpec((tm, tn), lambda i, j: (i, j)),
        out_shape=jax.ShapeDtypeStruct((m, ncols), BF16),
        compiler_params=_params("parallel", "parallel"),
        name=name,
    )(x, w, g.reshape(1, gdim))


def _proj_qk_kernel(x_ref, w_ref, g_ref, cos_ref, sa_ref, sb_ref, o_ref, *, rope_dim):
    half = rope_dim // 2
    for r in range(x_ref.shape[0] // ROW_CHUNK):
        rows = slice(r * ROW_CHUNK, (r + 1) * ROW_CHUNK)
        acc = jnp.dot(x_ref[rows, :], w_ref[...], preferred_element_type=F32)
        for c in range(acc.shape[1] // LANES):
            sl = slice(c * LANES, (c + 1) * LANES)
            y = _group_rmsnorm(acc[:, sl], g_ref[:, sl])
            y = (y * cos_ref[rows, :] + pltpu.roll(y, LANES - half, 1) * sa_ref[rows, :]
                 + pltpu.roll(y, half, 1) * sb_ref[rows, :])
            o_ref[rows, sl] = y.astype(o_ref.dtype)


def _proj_qk(x, w, g_cols, cos_t, sa_t, sb_t, seq, rope_dim, tm, tn):
    m, k = x.shape
    ncols = g_cols.shape[0]
    nseq = seq // tm
    tab = pl.BlockSpec((tm, LANES), lambda i, j: (i % nseq, 0))
    return pl.pallas_call(
        functools.partial(_proj_qk_kernel, rope_dim=rope_dim),
        grid=(m // tm, ncols // tn),
        in_specs=[pl.BlockSpec((tm, k), lambda i, j: (i, 0)),
                  pl.BlockSpec((k, tn), lambda i, j: (0, j)),
                  pl.BlockSpec((1, tn), lambda i, j: (0, j)),
                  tab, tab, tab],
        out_specs=pl.BlockSpec((tm, tn), lambda i, j: (i, j)),
        out_shape=jax.ShapeDtypeStruct((m, ncols), BF16),
        compiler_params=_params("parallel", "parallel"),
        name="proj_qk",
    )(x, w, g_cols.reshape(1, ncols), cos_t, sa_t, sb_t)


def _diff_attn_kernel(q_ref, k_ref, v_ref, lam_ref, gs_ref, o_ref,
                      vt_scr, s_scr, p_scr, a_scr, m_scr, l_scr, acc_scr, *, tq, tk, dk, lambda_init):
    i = pl.program_id(2)
    c0 = dk ** -0.5 * LOG2E
    neg = -1e30

    @pl.when(i == 0)
    def _():
        for blk in range(vt_scr.shape[0]):
            vt_scr[blk] = v_ref[blk * tk:(blk + 1) * tk, :].T

    q = q_ref[...]
    m_scr[...] = jnp.full(m_scr.shape, neg, F32)
    l_scr[...] = jnp.zeros(l_scr.shape, F32)
    acc_scr[...] = jnp.zeros(acc_scr.shape, F32)

    def scores(blk, slot, qlo=0):
        k = k_ref[pl.ds(pl.multiple_of(blk * tk, tk), tk), :]
        for c in range(2):
            sl = slice(c * dk, (c + 1) * dk)
            s_scr[slot, c, :, qlo:] = lax.dot_general(k[:, sl], q[qlo:, sl], (((1,), (1,)), ((), ())),
                                                      preferred_element_type=F32)

    def softmax(slot, diag=None, qlo=0):
        if diag is not None:
            key_chunk = (diag * tk + lax.broadcasted_iota(jnp.int32, (tk, tq - qlo), 0)) // CHUNK
            q_chunk = (qlo + lax.broadcasted_iota(jnp.int32, (tk, tq - qlo), 1)) // CHUNK
            allowed = key_chunk <= q_chunk
        for c in range(2):
            s = s_scr[slot, c, :, qlo:]
            if diag is not None:
                s = jnp.where(allowed, s, neg)
            m_prev = m_scr[c, :, qlo:]
            m_new = jnp.maximum(m_prev, jnp.max(s, axis=0, keepdims=True))
            alpha = jnp.exp2((m_prev - m_new) * c0)
            p = jnp.exp2((s - m_new) * c0)
            l_scr[c, :, qlo:] = alpha * l_scr[c, :, qlo:] + jnp.sum(p, axis=0, keepdims=True)
            p_scr[slot, c, :, qlo:] = p.astype(BF16)
            a_scr[slot, c, :, qlo:] = alpha
            m_scr[c, :, qlo:] = m_new

    def values(slot, blk, qlo=0):
        vt = vt_scr[blk]
        for c in range(2):
            acc_scr[c, :, qlo:] = (a_scr[slot, c, :, qlo:] * acc_scr[c, :, qlo:]
                                   + jnp.dot(vt, p_scr[slot, c, :, qlo:], preferred_element_type=F32))

    scores(2 * i + 1, 0, tk)
    scores(2 * i, 1)
    softmax(0, diag=1, qlo=tk)
    scores(0, 0)
    softmax(1, diag=0)
    values(0, 2 * i + 1, tk)

    def pair(j, last):
        scores(2 * j - 1, 1)
        softmax(0)
        values(1, jnp.where(j == 1, 2 * i, 2 * j - 3))
        if not last:
            scores(2 * j, 0)
        softmax(1)
        values(0, 2 * j - 2)

    def body(j, carry):
        pair(j, False)
        return carry

    lax.fori_loop(1, i, body, 0)

    @pl.when(i >= 1)
    def _():
        pair(i, True)

    values(1, jnp.where(i == 0, 0, 2 * i - 1))

    lam_v = lam_ref[...]
    lam = (jnp.exp(jnp.sum(lam_v[0:1] * lam_v[1:2], axis=-1, keepdims=True))
           - jnp.exp(jnp.sum(lam_v[2:3] * lam_v[3:4], axis=-1, keepdims=True)) + lambda_init)
    o = acc_scr[0] / l_scr[0] - lam * (acc_scr[1] / l_scr[1])
    ms = jnp.mean(o * o, axis=0, keepdims=True)
    y = o * lax.rsqrt(ms + EPS) * gs_ref[...] * (1.0 - lambda_init)
    o_ref[...] = y.T.astype(o_ref.dtype)


def _diff_attn(qk, v, lam_vecs, g_subln, batch, seq, heads, dk, lambda_init, tq):
    dv = 2 * dk
    tk = tq // 2
    nq = seq // tq
    kernel = functools.partial(_diff_attn_kernel, tq=tq, tk=tk, dk=dk, lambda_init=lambda_init)
    return pl.pallas_call(
        kernel,
        grid=(batch, heads, nq),
        in_specs=[pl.BlockSpec((tq, dv), lambda b, h, i: (b * nq + i, h)),
                  pl.BlockSpec((seq, dv), lambda b, h, i: (b, heads + h)),
                  pl.BlockSpec((seq, dv), lambda b, h, i: (b, h)),
                  pl.BlockSpec((4, dk), lambda b, h, i: (0, 0)),
                  pl.BlockSpec((dv, 1), lambda b, h, i: (0, 0))],
        out_specs=pl.BlockSpec((tq, dv), lambda b, h, i: (b * nq + i, h)),
        out_shape=jax.ShapeDtypeStruct((batch * seq, heads * dv), BF16),
        scratch_shapes=[pltpu.VMEM((seq // tk, dv, tk), BF16),
                        pltpu.VMEM((2, 2, tk, tq), F32), pltpu.VMEM((2, 2, tk, tq), BF16),
                        pltpu.VMEM((2, 2, 1, tq), F32), pltpu.VMEM((2, 1, tq), F32),
                        pltpu.VMEM((2, 1, tq), F32), pltpu.VMEM((2, dv, tq), F32)],
        compiler_params=_params("parallel", "parallel", "arbitrary"),
        name="diff_attn",
    )(qk, qk, v, lam_vecs, g_subln.reshape(dv, 1))


def _pool_kernel(up_ref, prev_ref, wp_ref, ps_ref, o_ref, ext_scr, *, tm, nseq, halo, gdim):
    i = pl.program_id(0)
    ti = i % nseq
    ext_scr[halo:halo + tm, :] = up_ref[...]
    ext_scr[0:halo, :] = jnp.where(ti == 0, 0.0, prev_ref[...])
    t1 = (ti * tm + 1 + lax.broadcasted_iota(jnp.int32, (tm, 1), 0)).astype(F32)
    for g, win in enumerate(POOL_WINDOWS):
        sl = slice(g * gdim, (g + 1) * gdim)
        cur = ext_scr[halo:halo + tm, sl]
        tot = cur
        for d in range(1, win):
            tot = tot + ext_scr[halo - d:halo - d + tm, sl]
        y = tot / jnp.minimum(t1, float(win)) - cur
        z = jnp.dot(y.astype(BF16), wp_ref[g], preferred_element_type=F32) * ps_ref[g:g + 1, :]
        o_ref[:, sl] = z.astype(o_ref.dtype)


def _pool_mixer(up, w_pool, pool_scale, seq, tm):
    m, width = up.shape
    groups, gdim, _ = w_pool.shape
    halo = 2 * SUBLANES
    assert max(POOL_WINDOWS) <= halo
    nseq = seq // tm
    per = tm // halo
    kernel = functools.partial(_pool_kernel, tm=tm, nseq=nseq, halo=halo, gdim=gdim)
    return pl.pallas_call(
        kernel,
        grid=(m // tm,),
        in_specs=[pl.BlockSpec((tm, width), lambda i: (i, 0)),
                  pl.BlockSpec((halo, width), lambda i: (jnp.maximum(i * per - 1, 0), 0)),
                  pl.BlockSpec((groups, gdim, gdim), lambda i: (0, 0, 0)),
                  pl.BlockSpec((groups, gdim), lambda i: (0, 0))],
        out_specs=pl.BlockSpec((tm, width), lambda i: (i, 0)),
        out_shape=jax.ShapeDtypeStruct((m, width), BF16),
        scratch_shapes=[pltpu.VMEM((tm + halo, width), F32)],
        compiler_params=_params("parallel"),
        name="pool_mixer",
    )(up, up, w_pool, pool_scale)


def _mem_attn_kernel(q_ref, k_ref, v_ref, o_ref, *, heads, hd):
    scale = hd ** -0.5
    for h in range(heads):
        sl = slice(h * hd, (h + 1) * hd)
        s = lax.dot_general(q_ref[:, sl], k_ref[:, sl], (((1,), (1,)), ((), ())),
                            preferred_element_type=F32) * scale
        p = jnp.exp(s - jnp.max(s, axis=-1, keepdims=True))
        l = jnp.sum(p, axis=-1, keepdims=True)
        o = jnp.dot(p.astype(BF16), v_ref[:, sl], preferred_element_type=F32) / l
        o_ref[:, sl] = o.astype(o_ref.dtype)


def _mem_attn(qm, km, vm, batch, seq, mem_len, heads, tq):
    m, width = qm.shape
    nq = seq // tq
    kernel = functools.partial(_mem_attn_kernel, heads=heads, hd=width // heads)
    return pl.pallas_call(
        kernel,
        grid=(batch, nq),
        in_specs=[pl.BlockSpec((tq, width), lambda b, i: (b * nq + i, 0)),
                  pl.BlockSpec((mem_len, width), lambda b, i: (b, 0)),
                  pl.BlockSpec((mem_len, width), lambda b, i: (b, 0))],
        out_specs=pl.BlockSpec((tq, width), lambda b, i: (b * nq + i, 0)),
        out_shape=jax.ShapeDtypeStruct((m, width), BF16),
        compiler_params=_params("parallel", "parallel"),
        name="mem_attn",
    )(qm, km, vm)


def _merge_kernel(ya_ref, yb_ref, yc_ref, wa_ref, wb_ref, wc_ref, ga_ref, gb_ref, gc_ref, o_ref):
    out = jax.nn.sigmoid(ga_ref[...]) * jnp.dot(ya_ref[...], wa_ref[...], preferred_element_type=F32)
    out = out + jax.nn.sigmoid(gb_ref[...]) * jnp.dot(yb_ref[...], wb_ref[...],
                                                      preferred_element_type=F32)
    out = out + jax.nn.sigmoid(gc_ref[...]) * jnp.dot(yc_ref[...], wc_ref[...],
                                                      preferred_element_type=F32)
    o_ref[...] = out.astype(o_ref.dtype)


def _merge(ya, yb, yc, w_a, w_b, w_c, gl, tm, tn):
    m = ya.shape[0]
    d = w_a.shape[1]
    nj = d // tn
    row = lambda a: pl.BlockSpec((tm, a.shape[1]), lambda i, j: (i, 0))
    col = lambda w: pl.BlockSpec((w.shape[0], tn), lambda i, j: (0, j))
    gate = lambda b: pl.BlockSpec((tm, tn), lambda i, j: (i, b * nj + j))
    return pl.pallas_call(
        _merge_kernel,
        grid=(m // tm, nj),
        in_specs=[row(ya), row(yb), row(yc), col(w_a), col(w_b), col(w_c), gate(0), gate(1), gate(2)],
        out_specs=pl.BlockSpec((tm, tn), lambda i, j: (i, j)),
        out_shape=jax.ShapeDtypeStruct((m, d), BF16),
        compiler_params=_params("parallel", "parallel"),
        name="merge",
    )(ya, yb, yc, w_a, w_b, w_c, gl, gl, gl)


def _proj_residual_kernel(a_ref, w_ref, r_ref, o_ref):
    o_ref[...] = r_ref[...] + jnp.dot(a_ref[...], w_ref[...], preferred_element_type=F32)


def _proj_residual(a, w, res, tm, tn):
    m, k = a.shape
    n = w.shape[1]
    return pl.pallas_call(
        _proj_residual_kernel,
        grid=(m // tm, n // tn),
        in_specs=[pl.BlockSpec((tm, k), lambda i, j: (i, 0)),
                  pl.BlockSpec((k, tn), lambda i, j: (0, j)),
                  pl.BlockSpec((tm, tn), lambda i, j: (i, j))],
        out_specs=pl.BlockSpec((tm, tn), lambda i, j: (i, j)),
        out_shape=jax.ShapeDtypeStruct((m, n), F32),
        compiler_params=_params("parallel", "parallel"),
        name="proj_residual",
    )(a, w, res)


def _ffn_up_kernel(h_ref, wg_ref, wv_ref, cwg_ref, cwv_ref, cbg_ref, cbv_ref, o_ref, yg_scr, yv_scr,
                   *, tm, nseq):
    i = pl.program_id(1)
    pad = SUBLANES

    @pl.when(i % nseq == 0)
    def _():
        yg_scr[0:pad, :] = jnp.zeros((pad, yg_scr.shape[1]), F32)
        yv_scr[0:pad, :] = jnp.zeros((pad, yv_scr.shape[1]), F32)

    def conv(r, y_scr, w_ref, cw_ref, cb_ref):
        lo = pad + r * ROW_CHUNK
        y_scr[lo:lo + ROW_CHUNK, :] = jnp.dot(h_ref[r * ROW_CHUNK:(r + 1) * ROW_CHUNK, :], w_ref[...],
                                              preferred_element_type=F32)
        u = cb_ref[...] + y_scr[lo:lo + ROW_CHUNK, :] * cw_ref[CONV_W - 1:CONV_W, :]
        for d in range(1, CONV_W):
            u = u + y_scr[lo - d:lo - d + ROW_CHUNK, :] * cw_ref[CONV_W - 1 - d:CONV_W - d, :]
        return u

    for r in range(tm // ROW_CHUNK):
        gate = conv(r, yg_scr, wg_ref, cwg_ref, cbg_ref)
        val = conv(r, yv_scr, wv_ref, cwv_ref, cbv_ref)
        o_ref[r * ROW_CHUNK:(r + 1) * ROW_CHUNK, :] = (gate * jax.nn.sigmoid(gate) * val).astype(o_ref.dtype)
    yg_scr[0:pad, :] = yg_scr[tm:tm + pad, :]
    yv_scr[0:pad, :] = yv_scr[tm:tm + pad, :]


def _ffn_up(h, w_up, conv_w, conv_b, seq, tm, tn):
    m, k = h.shape
    ff = w_up.shape[1] // 2
    nj = ff // tn
    nseq = seq // tm
    kernel = functools.partial(_ffn_up_kernel, tm=tm, nseq=nseq)
    wspec = lambda half: pl.BlockSpec((k, tn), lambda j, i: (0, half * nj + j))
    cwspec = lambda half: pl.BlockSpec((CONV_W, tn), lambda j, i: (0, half * nj + j))
    cbspec = lambda half: pl.BlockSpec((1, tn), lambda j, i: (0, half * nj + j))
    return pl.pallas_call(
        kernel,
        grid=(nj, m // tm),
        in_specs=[pl.BlockSpec((tm, k), lambda j, i: (i, 0)),
                  wspec(0), wspec(1), cwspec(0), cwspec(1), cbspec(0), cbspec(1)],
        out_specs=pl.BlockSpec((tm, tn), lambda j, i: (i, j)),
        out_shape=jax.ShapeDtypeStruct((m, ff), BF16),
        scratch_shapes=[pltpu.VMEM((tm + SUBLANES, tn), F32), pltpu.VMEM((tm + SUBLANES, tn), F32)],
        compiler_params=_params("arbitrary", "arbitrary"),
        name="ffn_up",
    )(h, w_up, w_up, conv_w, conv_w, conv_b.reshape(1, -1), conv_b.reshape(1, -1))


def _ffn_down_kernel(a_ref, w_ref, r_ref, o_ref, acc_scr):
    kk = pl.program_id(2)

    @pl.when(kk == 0)
    def _():
        acc_scr[...] = r_ref[...]

    acc_scr[...] += jnp.dot(a_ref[...], w_ref[...], preferred_element_type=F32)

    @pl.when(kk == pl.num_programs(2) - 1)
    def _():
        o_ref[...] = acc_scr[...]


def _ffn_down(a, w, res, tm, tn, tk):
    m, k = a.shape
    n = w.shape[1]
    return pl.pallas_call(
        _ffn_down_kernel,
        grid=(m // tm, n // tn, k // tk),
        in_specs=[pl.BlockSpec((tm, tk), lambda i, j, kk: (i, kk)),
                  pl.BlockSpec((tk, tn), lambda i, j, kk: (kk, j)),
                  pl.BlockSpec((tm, tn), lambda i, j, kk: (i, j))],
        out_specs=pl.BlockSpec((tm, tn), lambda i, j, kk: (i, j)),
        out_shape=jax.ShapeDtypeStruct((m, n), F32),
        scratch_shapes=[pltpu.VMEM((tm, tn), F32)],
        compiler_params=_params("parallel", "parallel", "arbitrary"),
        name="ffn_down",
    )(a, w, res)


def _rope_tables(seq, rope_dim):
    half = rope_dim // 2
    pos = jnp.arange(seq, dtype=F32)
    inv = ROPE_THETA ** (-jnp.arange(0, rope_dim, 2, dtype=F32) / rope_dim)
    ang = pos[:, None] * inv[None, :]
    cos, sin = jnp.cos(ang), jnp.sin(ang)
    ones = jnp.ones((seq, LANES - rope_dim), F32)
    zeros = jnp.zeros((seq, LANES - rope_dim), F32)
    zh = jnp.zeros((seq, half), F32)
    cos_t = jnp.concatenate([cos, cos, ones], axis=1)
    sa_t = jnp.concatenate([-sin, zh, zeros], axis=1)
    sb_t = jnp.concatenate([zh, sin, zeros], axis=1)
    return cos_t, sa_t, sb_t


def _pad_halves(a, ff, ff_pad):
    pad = [(0, 0)] * (a.ndim - 1) + [(0, ff_pad - ff)]
    return jnp.concatenate([jnp.pad(a[..., :ff], pad), jnp.pad(a[..., ff:], pad)], axis=-1)


def kernel(x, mem, g_attn_norm, w_in, g_qa, g_ka, lam_q1, lam_k1, lam_q2, lam_k2, g_subln, w_pool,
           pool_scale, g_mem, w_mkv, g_qm, g_km, w_a, w_b, w_c, w_o, g_ffn_norm, w_up, conv_w, conv_b,
           w_down):
    batch, seq, d_model = x.shape
    mem_len = mem.shape[1]
    depth = w_in.shape[0]
    dk = g_qa.shape[1]
    dv = g_subln.shape[1]
    heads = w_a.shape[1] // dv
    qk_width = heads * 2 * dk
    pool_width = w_b.shape[1]
    mem_width = w_c.shape[1]
    mem_hd = g_qm.shape[1]
    rope_dim = dk // 4
    d_ff = w_down.shape[1]
    ff_pad = -(-d_ff // 1024) * 1024
    assert dk == LANES and dv == 2 * dk

    m = batch * seq
    xf = x.reshape(m, d_model)
    cos_t, sa_t, sb_t = _rope_tables(seq, rope_dim)
    c_v = 2 * qk_width
    c_up = c_v + heads * dv
    c_qm = c_up + pool_width
    c_gl = c_qm + mem_width

    for l in range(depth):
        lambda_init = 0.8 - 0.6 * math.exp(-0.3 * l)
        w_in_b = w_in[l].astype(BF16)
        g_qk = jnp.concatenate([jnp.tile(g_qa[l], qk_width // dk), jnp.tile(g_ka[l], qk_width // dk)])
        lam_vecs = jnp.stack([lam_q1[l], lam_k1[l], lam_q2[l], lam_k2[l]]).astype(F32)

        h = _rmsnorm(xf, g_attn_norm[l], 256)
        qk = _proj_qk(h, w_in_b, g_qk, cos_t, sa_t, sb_t, seq, rope_dim, 1024, 1024)
        attn_tile = 1024
        v = _proj_plain(h, w_in_b, c_v, heads * dv, BF16, 1024, 1024, "proj_v")
        up = _proj_plain(h, w_in_b, c_up, pool_width, F32, 1024, 1024, "proj_up")
        qm = _proj_norm(h, w_in_b, g_qm[l], c_qm, mem_width, 1024, 1024, "proj_qm")
        gl = _proj_plain(h, w_in_b, c_gl, 3 * d_model, F32, 1024, 1024, "proj_gates")

        ya = _diff_attn(qk, v, lam_vecs, g_subln[l], batch, seq, heads, dk, lambda_init, attn_tile)
        yb = _pool_mixer(up, w_pool[l].astype(BF16), pool_scale[l], seq, 512)

        mem_n = _rmsnorm(mem.reshape(batch * mem_len, d_model), g_mem[l], 256)
        w_mkv_b = w_mkv[l].astype(BF16)
        km = _proj_norm(mem_n, w_mkv_b, g_km[l], 0, mem_width, batch * mem_len, 1024, "proj_km")
        vm = _proj_plain(mem_n, w_mkv_b, mem_width, mem_width, BF16, batch * mem_len, 1024, "proj_vm")
        yc = _mem_attn(qm, km, vm, batch, seq, mem_len, mem_width // mem_hd, 512)

        merged = _merge(ya, yb, yc, w_a[l].astype(BF16), w_b[l].astype(BF16), w_c[l].astype(BF16),
                        gl, 1024, 512)
        xf = _proj_residual(merged, w_o[l].astype(BF16), xf, 1024, 512)

        h2 = _rmsnorm(xf, g_ffn_norm[l], 256)
        w_up_b = _pad_halves(w_up[l].astype(BF16), d_ff, ff_pad)
        act = _ffn_up(h2, w_up_b, _pad_halves(conv_w[l], d_ff, ff_pad),
                      _pad_halves(conv_b[l], d_ff, ff_pad), seq, 1024, 512)
        w_down_b = jnp.pad(w_down[l].astype(BF16), ((0, ff_pad - d_ff), (0, 0)))
        xf = _ffn_down(act, w_down_b, xf, 1024, 1024, ff_pad // 4)

    return xf.reshape(batch, seq, d_model)
```

```python
import functools
import math
from typing import Callable, NamedTuple

import jax
import jax.numpy as jnp
from jax import lax
from jax.experimental import pallas as pl
from jax.experimental.pallas import tpu as pltpu

F32 = jnp.float32
BF16 = jnp.bfloat16

EPS = 1e-6
ROPE_THETA = 500000.0
CHUNK = 64
POOL_WINDOWS = (2, 4, 8, 16)
CONV_W = 3

LANES = 128
SUBLANES = 8
VMEM_LIMIT_BYTES = 56 * 1024 * 1024
ROW_CHUNK = 256
LOG2E = 1.4426950408889634


def _params(*sem):
    return pltpu.CompilerParams(dimension_semantics=sem, vmem_limit_bytes=VMEM_LIMIT_BYTES)


def _group_rmsnorm(x, g):
    ms = jnp.mean(x * x, axis=-1, keepdims=True)
    return x * lax.rsqrt(ms + EPS) * g


class _Rider(NamedTuple):
    src: jax.Array
    in_block: tuple
    in_index: Callable
    out_shape: tuple
    out_block: tuple
    out_index: Callable
    nblocks: int
    body: Callable


def _rider_io(rider, step_of, nsteps):
    assert rider.nblocks <= nsteps, "the host call has too few grid steps for this rider"
    block_of = lambda *g: jnp.minimum(step_of(*g), rider.nblocks - 1)
    in_spec = pl.BlockSpec(rider.in_block, lambda *g: rider.in_index(block_of(*g)))
    out_spec = pl.BlockSpec(rider.out_block, lambda *g: rider.out_index(block_of(*g)))
    return in_spec, out_spec, jax.ShapeDtypeStruct(rider.out_shape, BF16)


def _run_rider(nblocks, body, step, src_ref, dst_ref):
    @pl.when(step < nblocks)
    def _():
        body(src_ref, dst_ref)


def _cast_body(src_ref, dst_ref):
    dst_ref[...] = src_ref[...].astype(dst_ref.dtype)


def _pad_halves_body(src_ref, dst_ref):
    ff = src_ref.shape[1] // 2
    ff_pad = dst_ref.shape[1] // 2
    zeros = jnp.zeros((dst_ref.shape[0], ff_pad - ff), dst_ref.dtype)
    for half in range(2):
        dst_ref[:, half * ff_pad:half * ff_pad + ff] = src_ref[:, half * ff:(half + 1) * ff].astype(dst_ref.dtype)
        dst_ref[:, half * ff_pad + ff:(half + 1) * ff_pad] = zeros


def _row_block_rider(src, tr, out_cols, body):
    rows, cols = src.shape
    assert rows % tr == 0
    return _Rider(src, (tr, cols), lambda n: (n, 0), (rows, out_cols), (tr, out_cols), lambda n: (n, 0),
                  rows // tr, body)


def _rmsnorm_kernel(x_ref, g_ref, o_ref):
    o_ref[...] = _group_rmsnorm(x_ref[...], g_ref[...]).astype(o_ref.dtype)


def _rmsnorm(x, g, tm):
    m, d = x.shape
    return pl.pallas_call(
        _rmsnorm_kernel,
        grid=(m // tm,),
        in_specs=[pl.BlockSpec((tm, d), lambda i: (i, 0)), pl.BlockSpec((1, d), lambda i: (0, 0))],
        out_specs=pl.BlockSpec((tm, d), lambda i: (i, 0)),
        out_shape=jax.ShapeDtypeStruct((m, d), BF16),
        compiler_params=_params("parallel"),
        name="rmsnorm",
    )(x, g.reshape(1, d))


def _proj_plain(x, w, col0, ncols, out_dtype, tm, tn, name, rider=None):
    m, k = x.shape
    j0 = col0 // tn
    nj = ncols // tn
    rider_args = None if rider is None else (rider.nblocks, rider.body)

    def kernel(x_ref, w_ref, *refs):
        o_ref = refs[-1] if rider_args is None else refs[1]
        o_ref[...] = jnp.dot(x_ref[...], w_ref[...], preferred_element_type=F32).astype(o_ref.dtype)
        if rider_args is not None:
            _run_rider(*rider_args, pl.program_id(0) * nj + pl.program_id(1), refs[0], refs[2])

    in_specs = [pl.BlockSpec((tm, k), lambda i, j: (i, 0)),
                pl.BlockSpec((k, tn), lambda i, j: (0, j0 + j))]
    out_specs = [pl.BlockSpec((tm, tn), lambda i, j: (i, j))]
    out_shape = [jax.ShapeDtypeStruct((m, ncols), out_dtype)]
    args = [x, w]
    if rider is not None:
        r_in, r_out, r_shape = _rider_io(rider, lambda i, j: i * nj + j, (m // tm) * nj)
        in_specs.append(r_in)
        out_specs.append(r_out)
        out_shape.append(r_shape)
        args.append(rider.src)
    sem = ("parallel", "parallel") if rider is None else ("arbitrary", "arbitrary")
    outs = pl.pallas_call(
        kernel,
        grid=(m // tm, nj),
        in_specs=in_specs,
        out_specs=out_specs,
        out_shape=out_shape,
        compiler_params=_params(*sem),
        name=name,
    )(*args)
    return outs[0] if rider is None else tuple(outs)


def _proj_norm_kernel(x_ref, w_ref, g_ref, o_ref, *, gdim):
    acc = jnp.dot(x_ref[...], w_ref[...], preferred_element_type=F32)
    for c in range(acc.shape[1] // gdim):
        sl = slice(c * gdim, (c + 1) * gdim)
        o_ref[:, sl] = _group_rmsnorm(acc[:, sl], g_ref[...]).astype(o_ref.dtype)


def _proj_norm(x, w, g, col0, ncols, tm, tn, name):
    m, k = x.shape
    gdim = g.shape[0]
    j0 = col0 // tn
    return pl.pallas_call(
        functools.partial(_proj_norm_kernel, gdim=gdim),
        grid=(m // tm, ncols // tn),
        in_specs=[pl.BlockSpec((tm, k), lambda i, j: (i, 0)),
                  pl.BlockSpec((k, tn), lambda i, j: (0, j0 + j)),
                  pl.BlockSpec((1, gdim), lambda i, j: (0, 0))],
        out_specs=pl.BlockSpec((tm, tn), lambda i, j: (i, j)),
        out_shape=jax.ShapeDtypeStruct((m, ncols), BF16),
        compiler_params=_params("parallel", "parallel"),
        name=name,
    )(x, w, g.reshape(1, gdim))


def _proj_qk(x, w, g_cols, cos_t, sa_t, sb_t, seq, rope_dim, tm, tn, rider):
    m, k = x.shape
    ncols = g_cols.shape[0]
    nseq = seq // tm
    nj = ncols // tn
    half = rope_dim // 2
    rider_args = (rider.nblocks, rider.body)

    def kernel(x_ref, w_ref, g_ref, cos_ref, sa_ref, sb_ref, rsrc_ref, o_ref, rdst_ref):
        for r in range(tm // ROW_CHUNK):
            rows = slice(r * ROW_CHUNK, (r + 1) * ROW_CHUNK)
            acc = jnp.dot(x_ref[rows, :], w_ref[...], preferred_element_type=F32)
            for c in range(tn // LANES):
                sl = slice(c * LANES, (c + 1) * LANES)
                y = _group_rmsnorm(acc[:, sl], g_ref[:, sl])
                y = (y * cos_ref[rows, :] + pltpu.roll(y, LANES - half, 1) * sa_ref[rows, :]
                     + pltpu.roll(y, half, 1) * sb_ref[rows, :])
                o_ref[rows, sl] = y.astype(o_ref.dtype)
        _run_rider(*rider_args, pl.program_id(0) * nj + pl.program_id(1), rsrc_ref, rdst_ref)

    tab = pl.BlockSpec((tm, LANES), lambda i, j: (i % nseq, 0))
    r_in, r_out, r_shape = _rider_io(rider, lambda i, j: i * nj + j, (m // tm) * nj)
    return pl.pallas_call(
        kernel,
        grid=(m // tm, nj),
        in_specs=[pl.BlockSpec((tm, k), lambda i, j: (i, 0)),
                  pl.BlockSpec((k, tn), lambda i, j: (0, j)),
                  pl.BlockSpec((1, tn), lambda i, j: (0, j)),
                  tab, tab, tab, r_in],
        out_specs=[pl.BlockSpec((tm, tn), lambda i, j: (i, j)), r_out],
        out_shape=[jax.ShapeDtypeStruct((m, ncols), BF16), r_shape],
        compiler_params=_params("arbitrary", "arbitrary"),
        name="proj_qk",
    )(x, w, g_cols.reshape(1, ncols), cos_t, sa_t, sb_t, rider.src)


def _diff_attn_kernel(q_ref, k_ref, v_ref, lam_ref, gs_ref, rsrc_ref, o_ref, rdst_ref,
                      vt_scr, s_scr, p_scr, a_scr, m_scr, l_scr, acc_scr,
                      *, tq, tk, dk, lambda_init, rider_args):
    i = pl.program_id(2)
    c0 = dk ** -0.5 * LOG2E
    neg = -1e30
    step = (pl.program_id(0) * pl.num_programs(1) + pl.program_id(1)) * pl.num_programs(2) + i
    _run_rider(*rider_args, step, rsrc_ref, rdst_ref)

    @pl.when(i == 0)
    def _():
        for blk in range(vt_scr.shape[0]):
            vt_scr[blk] = v_ref[blk * tk:(blk + 1) * tk, :].T

    q = q_ref[...]
    m_scr[...] = jnp.full(m_scr.shape, neg, F32)
    l_scr[...] = jnp.zeros(l_scr.shape, F32)
    acc_scr[...] = jnp.zeros(acc_scr.shape, F32)

    def scores(blk, slot, qlo=0):
        k = k_ref[pl.ds(pl.multiple_of(blk * tk, tk), tk), :]
        for c in range(2):
            sl = slice(c * dk, (c + 1) * dk)
            s_scr[slot, c, :, qlo:] = lax.dot_general(k[:, sl], q[qlo:, sl], (((1,), (1,)), ((), ())),
                                                      preferred_element_type=F32)

    def softmax(slot, diag=None, qlo=0):
        if diag is not None:
            key_chunk = (diag * tk + lax.broadcasted_iota(jnp.int32, (tk, tq - qlo), 0)) // CHUNK
            q_chunk = (qlo + lax.broadcasted_iota(jnp.int32, (tk, tq - qlo), 1)) // CHUNK
            allowed = key_chunk <= q_chunk
        for c in range(2):
            s = s_scr[slot, c, :, qlo:]
            if diag is not None:
                s = jnp.where(allowed, s, neg)
            m_prev = m_scr[c, :, qlo:]
            m_new = jnp.maximum(m_prev, jnp.max(s, axis=0, keepdims=True))
            alpha = jnp.exp2((m_prev - m_new) * c0)
            p = jnp.exp2((s - m_new) * c0)
            l_scr[c, :, qlo:] = alpha * l_scr[c, :, qlo:] + jnp.sum(p, axis=0, keepdims=True)
            p_scr[slot, c, :, qlo:] = p.astype(BF16)
            a_scr[slot, c, :, qlo:] = alpha
            m_scr[c, :, qlo:] = m_new

    def values(slot, blk, qlo=0):
        vt = vt_scr[blk]
        for c in range(2):
            acc_scr[c, :, qlo:] = (a_scr[slot, c, :, qlo:] * acc_scr[c, :, qlo:]
                                   + jnp.dot(vt, p_scr[slot, c, :, qlo:], preferred_element_type=F32))

    scores(2 * i + 1, 0, tk)
    scores(2 * i, 1)
    softmax(0, diag=1, qlo=tk)
    scores(0, 0)
    softmax(1, diag=0)
    values(0, 2 * i + 1, tk)

    def pair(j, last):
        scores(2 * j - 1, 1)
        softmax(0)
        values(1, jnp.where(j == 1, 2 * i, 2 * j - 3))
        if not last:
            scores(2 * j, 0)
        softmax(1)
        values(0, 2 * j - 2)

    def body(j, carry):
        pair(j, False)
        return carry

    lax.fori_loop(1, i, body, 0)

    @pl.when(i >= 1)
    def _():
        pair(i, True)

    values(1, jnp.where(i == 0, 0, 2 * i - 1))

    lam_v = lam_ref[...]
    lam = (jnp.exp(jnp.sum(lam_v[0:1] * lam_v[1:2], axis=-1, keepdims=True))
           - jnp.exp(jnp.sum(lam_v[2:3] * lam_v[3:4], axis=-1, keepdims=True)) + lambda_init)
    o = acc_scr[0] / l_scr[0] - lam * (acc_scr[1] / l_scr[1])
    ms = jnp.mean(o * o, axis=0, keepdims=True)
    y = o * lax.rsqrt(ms + EPS) * gs_ref[...] * (1.0 - lambda_init)
    o_ref[...] = y.T.astype(o_ref.dtype)


def _diff_attn(qk, v, lam_vecs, g_subln, batch, seq, heads, dk, lambda_init, tq, rider):
    dv = 2 * dk
    tk = tq // 2
    nq = seq // tq
    kernel = functools.partial(_diff_attn_kernel, tq=tq, tk=tk, dk=dk, lambda_init=lambda_init,
                               rider_args=(rider.nblocks, rider.body))
    r_in, r_out, r_shape = _rider_io(rider, lambda b, h, i: (b * heads + h) * nq + i, batch * heads * nq)
    return pl.pallas_call(
        kernel,
        grid=(batch, heads, nq),
        in_specs=[pl.BlockSpec((tq, dv), lambda b, h, i: (b * nq + i, h)),
                  pl.BlockSpec((seq, dv), lambda b, h, i: (b, heads + h)),
                  pl.BlockSpec((seq, dv), lambda b, h, i: (b, h)),
                  pl.BlockSpec((4, dk), lambda b, h, i: (0, 0)),
                  pl.BlockSpec((dv, 1), lambda b, h, i: (0, 0)), r_in],
        out_specs=[pl.BlockSpec((tq, dv), lambda b, h, i: (b * nq + i, h)), r_out],
        out_shape=[jax.ShapeDtypeStruct((batch * seq, heads * dv), BF16), r_shape],
        scratch_shapes=[pltpu.VMEM((seq // tk, dv, tk), BF16),
                        pltpu.VMEM((2, 2, tk, tq), F32), pltpu.VMEM((2, 2, tk, tq), BF16),
                        pltpu.VMEM((2, 2, 1, tq), F32), pltpu.VMEM((2, 1, tq), F32),
                        pltpu.VMEM((2, 1, tq), F32), pltpu.VMEM((2, dv, tq), F32)],
        compiler_params=_params("arbitrary", "arbitrary", "arbitrary"),
        name="diff_attn",
    )(qk, qk, v, lam_vecs, g_subln.reshape(dv, 1), rider.src)


def _pool_kernel(up_ref, prev_ref, wp_ref, ps_ref, o_ref, ext_scr, *, tm, nseq, halo, gdim):
    i = pl.program_id(0)
    ti = i % nseq
    ext_scr[halo:halo + tm, :] = up_ref[...]
    ext_scr[0:halo, :] = jnp.where(ti == 0, 0.0, prev_ref[...])
    t1 = (ti * tm + 1 + lax.broadcasted_iota(jnp.int32, (tm, 1), 0)).astype(F32)
    for g, win in enumerate(POOL_WINDOWS):
        sl = slice(g * gdim, (g + 1) * gdim)
        cur = ext_scr[halo:halo + tm, sl]
        tot = cur
        for d in range(1, win):
            tot = tot + ext_scr[halo - d:halo - d + tm, sl]
        y = tot / jnp.minimum(t1, float(win)) - cur
        z = jnp.dot(y.astype(BF16), wp_ref[g], preferred_element_type=F32) * ps_ref[g:g + 1, :]
        o_ref[:, sl] = z.astype(o_ref.dtype)


def _pool_mixer(up, w_pool, pool_scale, seq, tm):
    m, width = up.shape
    groups, gdim, _ = w_pool.shape
    halo = 2 * SUBLANES
    assert max(POOL_WINDOWS) <= halo
    nseq = seq // tm
    per = tm // halo
    kernel = functools.partial(_pool_kernel, tm=tm, nseq=nseq, halo=halo, gdim=gdim)
    return pl.pallas_call(
        kernel,
        grid=(m // tm,),
        in_specs=[pl.BlockSpec((tm, width), lambda i: (i, 0)),
                  pl.BlockSpec((halo, width), lambda i: (jnp.maximum(i * per - 1, 0), 0)),
                  pl.BlockSpec((groups, gdim, gdim), lambda i: (0, 0, 0)),
                  pl.BlockSpec((groups, gdim), lambda i: (0, 0))],
        out_specs=pl.BlockSpec((tm, width), lambda i: (i, 0)),
        out_shape=jax.ShapeDtypeStruct((m, width), BF16),
        scratch_shapes=[pltpu.VMEM((tm + halo, width), F32)],
        compiler_params=_params("parallel"),
        name="pool_mixer",
    )(up, up, w_pool, pool_scale)


def _mem_attn_kernel(q_ref, k_ref, v_ref, o_ref, *, heads, hd):
    scale = hd ** -0.5
    for h in range(heads):
        sl = slice(h * hd, (h + 1) * hd)
        s = lax.dot_general(q_ref[:, sl], k_ref[:, sl], (((1,), (1,)), ((), ())),
                            preferred_element_type=F32) * scale
        p = jnp.exp(s - jnp.max(s, axis=-1, keepdims=True))
        l = jnp.sum(p, axis=-1, keepdims=True)
        o = jnp.dot(p.astype(BF16), v_ref[:, sl], preferred_element_type=F32) / l
        o_ref[:, sl] = o.astype(o_ref.dtype)


def _mem_attn(qm, km, vm, batch, seq, mem_len, heads, tq):
    m, width = qm.shape
    nq = seq // tq
    kernel = functools.partial(_mem_attn_kernel, heads=heads, hd=width // heads)
    return pl.pallas_call(
        kernel,
        grid=(batch, nq),
        in_specs=[pl.BlockSpec((tq, width), lambda b, i: (b * nq + i, 0)),
                  pl.BlockSpec((mem_len, width), lambda b, i: (b, 0)),
                  pl.BlockSpec((mem_len, width), lambda b, i: (b, 0))],
        out_specs=pl.BlockSpec((tq, width), lambda b, i: (b * nq + i, 0)),
        out_shape=jax.ShapeDtypeStruct((m, width), BF16),
        compiler_params=_params("parallel", "parallel"),
        name="mem_attn",
    )(qm, km, vm)


def _merge_kernel(ya_ref, yb_ref, yc_ref, wa_ref, wb_ref, wc_ref, ga_ref, gb_ref, gc_ref, o_ref):
    out = jax.nn.sigmoid(ga_ref[...]) * jnp.dot(ya_ref[...], wa_ref[...], preferred_element_type=F32)
    out = out + jax.nn.sigmoid(gb_ref[...]) * jnp.dot(yb_ref[...], wb_ref[...],
                                                      preferred_element_type=F32)
    out = out + jax.nn.sigmoid(gc_ref[...]) * jnp.dot(yc_ref[...], wc_ref[...],
                                                      preferred_element_type=F32)
    o_ref[...] = out.astype(o_ref.dtype)


def _merge(ya, yb, yc, w_a, w_b, w_c, gl, tm, tn):
    m = ya.shape[0]
    d = w_a.shape[1]
    nj = d // tn
    row = lambda a: pl.BlockSpec((tm, a.shape[1]), lambda i, j: (i, 0))
    col = lambda w: pl.BlockSpec((w.shape[0], tn), lambda i, j: (0, j))
    gate = lambda b: pl.BlockSpec((tm, tn), lambda i, j: (i, b * nj + j))
    return pl.pallas_call(
        _merge_kernel,
        grid=(m // tm, nj),
        in_specs=[row(ya), row(yb), row(yc), col(w_a), col(w_b), col(w_c), gate(0), gate(1), gate(2)],
        out_specs=pl.BlockSpec((tm, tn), lambda i, j: (i, j)),
        out_shape=jax.ShapeDtypeStruct((m, d), BF16),
        compiler_params=_params("parallel", "parallel"),
        name="merge",
    )(ya, yb, yc, w_a, w_b, w_c, gl, gl, gl)


def _proj_residual_kernel(a_ref, w_ref, r_ref, o_ref):
    o_ref[...] = r_ref[...] + jnp.dot(a_ref[...], w_ref[...], preferred_element_type=F32)


def _proj_residual(a, w, res, tm, tn):
    m, k = a.shape
    n = w.shape[1]
    return pl.pallas_call(
        _proj_residual_kernel,
        grid=(m // tm, n // tn),
        in_specs=[pl.BlockSpec((tm, k), lambda i, j: (i, 0)),
                  pl.BlockSpec((k, tn), lambda i, j: (0, j)),
                  pl.BlockSpec((tm, tn), lambda i, j: (i, j))],
        out_specs=pl.BlockSpec((tm, tn), lambda i, j: (i, j)),
        out_shape=jax.ShapeDtypeStruct((m, n), F32),
        compiler_params=_params("parallel", "parallel"),
        name="proj_residual",
    )(a, w, res)


def _ffn_up_kernel(h_ref, wg_ref, wv_ref, cwg_ref, cwv_ref, cbg_ref, cbv_ref, o_ref, yg_scr, yv_scr,
                   *, tm, nseq):
    i = pl.program_id(1)
    pad = SUBLANES

    @pl.when(i % nseq == 0)
    def _():
        yg_scr[0:pad, :] = jnp.zeros((pad, yg_scr.shape[1]), F32)
        yv_scr[0:pad, :] = jnp.zeros((pad, yv_scr.shape[1]), F32)

    def conv(r, y_scr, w_ref, cw_ref, cb_ref):
        lo = pad + r * ROW_CHUNK
        y_scr[lo:lo + ROW_CHUNK, :] = jnp.dot(h_ref[r * ROW_CHUNK:(r + 1) * ROW_CHUNK, :], w_ref[...],
                                              preferred_element_type=F32)
        u = cb_ref[...] + y_scr[lo:lo + ROW_CHUNK, :] * cw_ref[CONV_W - 1:CONV_W, :]
        for d in range(1, CONV_W):
            u = u + y_scr[lo - d:lo - d + ROW_CHUNK, :] * cw_ref[CONV_W - 1 - d:CONV_W - d, :]
        return u

    for r in range(tm // ROW_CHUNK):
        gate = conv(r, yg_scr, wg_ref, cwg_ref, cbg_ref)
        val = conv(r, yv_scr, wv_ref, cwv_ref, cbv_ref)
        o_ref[r * ROW_CHUNK:(r + 1) * ROW_CHUNK, :] = (gate * jax.nn.sigmoid(gate) * val).astype(o_ref.dtype)
    yg_scr[0:pad, :] = yg_scr[tm:tm + pad, :]
    yv_scr[0:pad, :] = yv_scr[tm:tm + pad, :]


def _ffn_up(h, w_up, conv_w, conv_b, seq, tm, tn):
    m, k = h.shape
    ff = w_up.shape[1] // 2
    nj = ff // tn
    nseq = seq // tm
    kernel = functools.partial(_ffn_up_kernel, tm=tm, nseq=nseq)
    wspec = lambda half: pl.BlockSpec((k, tn), lambda j, i: (0, half * nj + j))
    cwspec = lambda half: pl.BlockSpec((CONV_W, tn), lambda j, i: (0, half * nj + j))
    cbspec = lambda half: pl.BlockSpec((1, tn), lambda j, i: (0, half * nj + j))
    return pl.pallas_call(
        kernel,
        grid=(nj, m // tm),
        in_specs=[pl.BlockSpec((tm, k), lambda j, i: (i, 0)),
                  wspec(0), wspec(1), cwspec(0), cwspec(1), cbspec(0), cbspec(1)],
        out_specs=pl.BlockSpec((tm, tn), lambda j, i: (i, j)),
        out_shape=jax.ShapeDtypeStruct((m, ff), BF16),
        scratch_shapes=[pltpu.VMEM((tm + SUBLANES, tn), F32), pltpu.VMEM((tm + SUBLANES, tn), F32)],
        compiler_params=_params("arbitrary", "arbitrary"),
        name="ffn_up",
    )(h, w_up, w_up, conv_w, conv_w, conv_b.reshape(1, -1), conv_b.reshape(1, -1))


def _ffn_down_kernel(a_ref, w_ref, r_ref, o_ref, acc_scr, *, rem):
    kk = pl.program_id(2)
    last = pl.num_programs(2) - 1

    @pl.when(kk == 0)
    def _():
        acc_scr[...] = r_ref[...]

    @pl.when(kk < last)
    def _():
        acc_scr[...] += jnp.dot(a_ref[...], w_ref[...], preferred_element_type=F32)

    @pl.when(kk == last)
    def _():
        o_ref[...] = acc_scr[...] + jnp.dot(a_ref[:, :rem], w_ref[:rem, :], preferred_element_type=F32)


def _ffn_down(a, w, res, tm, tn, tk):
    m = a.shape[0]
    k, n = w.shape
    nk = pl.cdiv(k, tk)
    rem = k - (nk - 1) * tk
    assert tk % LANES == 0 and rem % LANES == 0 and a.shape[1] >= nk * tk
    return pl.pallas_call(
        functools.partial(_ffn_down_kernel, rem=rem),
        grid=(m // tm, n // tn, nk),
        in_specs=[pl.BlockSpec((tm, tk), lambda i, j, kk: (i, kk)),
                  pl.BlockSpec((tk, tn), lambda i, j, kk: (kk, j)),
                  pl.BlockSpec((tm, tn), lambda i, j, kk: (i, j))],
        out_specs=pl.BlockSpec((tm, tn), lambda i, j, kk: (i, j)),
        out_shape=jax.ShapeDtypeStruct((m, n), F32),
        scratch_shapes=[pltpu.VMEM((tm, tn), F32)],
        compiler_params=_params("parallel", "parallel", "arbitrary"),
        name="ffn_down",
    )(a, w, res)


def _rope_tables(seq, rope_dim):
    half = rope_dim // 2
    pos = jnp.arange(seq, dtype=F32)
    inv = ROPE_THETA ** (-jnp.arange(0, rope_dim, 2, dtype=F32) / rope_dim)
    ang = pos[:, None] * inv[None, :]
    cos, sin = jnp.cos(ang), jnp.sin(ang)
    ones = jnp.ones((seq, LANES - rope_dim), F32)
    zeros = jnp.zeros((seq, LANES - rope_dim), F32)
    zh = jnp.zeros((seq, half), F32)
    cos_t = jnp.concatenate([cos, cos, ones], axis=1)
    sa_t = jnp.concatenate([-sin, zh, zeros], axis=1)
    sb_t = jnp.concatenate([zh, sin, zeros], axis=1)
    return cos_t, sa_t, sb_t


def _pad_halves(a, ff, ff_pad):
    pad = [(0, 0)] * (a.ndim - 1) + [(0, ff_pad - ff)]
    return jnp.concatenate([jnp.pad(a[..., :ff], pad), jnp.pad(a[..., ff:], pad)], axis=-1)


def kernel(x, mem, g_attn_norm, w_in, g_qa, g_ka, lam_q1, lam_k1, lam_q2, lam_k2, g_subln, w_pool,
           pool_scale, g_mem, w_mkv, g_qm, g_km, w_a, w_b, w_c, w_o, g_ffn_norm, w_up, conv_w, conv_b,
           w_down):
    batch, seq, d_model = x.shape
    mem_len = mem.shape[1]
    depth = w_in.shape[0]
    dk = g_qa.shape[1]
    dv = g_subln.shape[1]
    heads = w_a.shape[1] // dv
    qk_width = heads * 2 * dk
    pool_width = w_b.shape[1]
    mem_width = w_c.shape[1]
    mem_hd = g_qm.shape[1]
    rope_dim = dk // 4
    d_ff = w_down.shape[1]
    ff_pad = -(-d_ff // 1024) * 1024
    assert dk == LANES and dv == 2 * dk

    m = batch * seq
    xf = x.reshape(m, d_model)
    cos_t, sa_t, sb_t = _rope_tables(seq, rope_dim)
    c_v = 2 * qk_width
    c_up = c_v + heads * dv
    c_qm = c_up + pool_width
    c_gl = c_qm + mem_width

    for l in range(depth):
        lambda_init = 0.8 - 0.6 * math.exp(-0.3 * l)
        g_qk = jnp.concatenate([jnp.tile(g_qa[l], qk_width // dk), jnp.tile(g_ka[l], qk_width // dk)])
        lam_vecs = jnp.stack([lam_q1[l], lam_k1[l], lam_q2[l], lam_k2[l]]).astype(F32)

        w_qk_b = w_in[l][:, :c_v].astype(BF16)
        nrest = (w_in.shape[2] - c_v) // c_v
        rest_rider = _Rider(w_in[l], (256, c_v), lambda n: (n // nrest, 1 + n % nrest),
                            (d_model, nrest * c_v), (256, c_v), lambda n: (n // nrest, n % nrest),
                            (d_model // 256) * nrest, _cast_body)

        h = _rmsnorm(xf, g_attn_norm[l], 256)
        qk, w_rest_b = _proj_qk(h, w_qk_b, g_qk, cos_t, sa_t, sb_t, seq, rope_dim, 1024, 1024, rest_rider)
        v = _proj_plain(h, w_rest_b, 0, heads * dv, BF16, 1024, 1024, "proj_v")
        up = _proj_plain(h, w_rest_b, c_up - c_v, pool_width, F32, 1024, 1024, "proj_up")
        qm = _proj_norm(h, w_rest_b, g_qm[l], c_qm - c_v, mem_width, 1024, 1024, "proj_qm")
        gl, w_down_b = _proj_plain(h, w_rest_b, c_gl - c_v, 3 * d_model, F32, 1024, 1024, "proj_gates",
                                   rider=_row_block_rider(w_down[l], 128, d_model, _cast_body))

        ya, w_up_b = _diff_attn(qk, v, lam_vecs, g_subln[l], batch, seq, heads, dk, lambda_init, 1024,
                                _row_block_rider(w_up[l], 32, 2 * ff_pad, _pad_halves_body))
        yb = _pool_mixer(up, w_pool[l].astype(BF16), pool_scale[l], seq, 512)

        mem_n = _rmsnorm(mem.reshape(batch * mem_len, d_model), g_mem[l], 256)
        w_mkv_b = w_mkv[l].astype(BF16)
        km = _proj_norm(mem_n, w_mkv_b, g_km[l], 0, mem_width, batch * mem_len, 1024, "proj_km")
        vm = _proj_plain(mem_n, w_mkv_b, mem_width, mem_width, BF16, batch * mem_len, 1024, "proj_vm")
        yc = _mem_attn(qm, km, vm, batch, seq, mem_len, mem_width // mem_hd, 512)

        merged = _merge(ya, yb, yc, w_a[l].astype(BF16), w_b[l].astype(BF16), w_c[l].astype(BF16),
                        gl, 1024, 512)
        xf = _proj_residual(merged, w_o[l].astype(BF16), xf, 1024, 512)

        h2 = _rmsnorm(xf, g_ffn_norm[l], 256)
        act = _ffn_up(h2, w_up_b, _pad_halves(conv_w[l], d_ff, ff_pad),
                      _pad_halves(conv_b[l], d_ff, ff_pad), seq, 1024, 512)
        xf = _ffn_down(act, w_down_b, xf, 1024, 1024, ff_pad // 4)

    return xf.reshape(batch, seq, d_model)
```

```python
import functools
import math
from typing import Callable, NamedTuple

import jax
import jax.numpy as jnp
from jax import lax
from jax.experimental import pallas as pl
from jax.experimental.pallas import tpu as pltpu

F32 = jnp.float32
BF16 = jnp.bfloat16

EPS = 1e-6
ROPE_THETA = 500000.0
CHUNK = 64
POOL_WINDOWS = (2, 4, 8, 16)
CONV_W = 3

LANES = 128
SUBLANES = 8
VMEM_LIMIT_BYTES = 56 * 1024 * 1024
ROW_CHUNK = 256
LOG2E = 1.4426950408889634


def _params(*sem):
    return pltpu.CompilerParams(dimension_semantics=sem, vmem_limit_bytes=VMEM_LIMIT_BYTES)


def _group_rmsnorm(x, g):
    ms = jnp.mean(x * x, axis=-1, keepdims=True)
    return x * lax.rsqrt(ms + EPS) * g


class _Rider(NamedTuple):
    src: jax.Array
    in_block: tuple
    in_index: Callable
    out_shape: tuple
    out_block: tuple
    out_index: Callable
    nblocks: int
    body: Callable


def _rider_io(rider, step_of, nsteps):
    assert rider.nblocks <= nsteps, "the host call has too few grid steps for this rider"
    block_of = lambda *g: jnp.minimum(step_of(*g), rider.nblocks - 1)
    in_spec = pl.BlockSpec(rider.in_block, lambda *g: rider.in_index(block_of(*g)))
    out_spec = pl.BlockSpec(rider.out_block, lambda *g: rider.out_index(block_of(*g)))
    return in_spec, out_spec, jax.ShapeDtypeStruct(rider.out_shape, BF16)


def _run_rider(nblocks, body, step, src_ref, dst_ref):
    @pl.when(step < nblocks)
    def _():
        body(src_ref, dst_ref)


def _cast_body(src_ref, dst_ref):
    dst_ref[...] = src_ref[...].astype(dst_ref.dtype)


def _pad_halves_body(src_ref, dst_ref):
    ff = src_ref.shape[1] // 2
    ff_pad = dst_ref.shape[1] // 2
    zeros = jnp.zeros((dst_ref.shape[0], ff_pad - ff), dst_ref.dtype)
    for half in range(2):
        dst_ref[:, half * ff_pad:half * ff_pad + ff] = src_ref[:, half * ff:(half + 1) * ff].astype(dst_ref.dtype)
        dst_ref[:, half * ff_pad + ff:(half + 1) * ff_pad] = zeros


def _row_block_rider(src, tr, out_cols, body):
    rows, cols = src.shape
    assert rows % tr == 0
    return _Rider(src, (tr, cols), lambda n: (n, 0), (rows, out_cols), (tr, out_cols), lambda n: (n, 0),
                  rows // tr, body)


def _rmsnorm_kernel(x_ref, g_ref, o_ref):
    o_ref[...] = _group_rmsnorm(x_ref[...], g_ref[...]).astype(o_ref.dtype)


def _rmsnorm(x, g, tm):
    m, d = x.shape
    return pl.pallas_call(
        _rmsnorm_kernel,
        grid=(m // tm,),
        in_specs=[pl.BlockSpec((tm, d), lambda i: (i, 0)), pl.BlockSpec((1, d), lambda i: (0, 0))],
        out_specs=pl.BlockSpec((tm, d), lambda i: (i, 0)),
        out_shape=jax.ShapeDtypeStruct((m, d), BF16),
        compiler_params=_params("parallel"),
        name="rmsnorm",
    )(x, g.reshape(1, d))


def _proj_plain(x, w, col0, ncols, out_dtype, tm, tn, name, rider=None):
    m, k = x.shape
    j0 = col0 // tn
    nj = ncols // tn
    rider_args = None if rider is None else (rider.nblocks, rider.body)

    def kernel(x_ref, w_ref, *refs):
        o_ref = refs[-1] if rider_args is None else refs[1]
        o_ref[...] = jnp.dot(x_ref[...], w_ref[...], preferred_element_type=F32).astype(o_ref.dtype)
        if rider_args is not None:
            _run_rider(*rider_args, pl.program_id(0) * nj + pl.program_id(1), refs[0], refs[2])

    in_specs = [pl.BlockSpec((tm, k), lambda i, j: (i, 0)),
                pl.BlockSpec((k, tn), lambda i, j: (0, j0 + j))]
    out_specs = [pl.BlockSpec((tm, tn), lambda i, j: (i, j))]
    out_shape = [jax.ShapeDtypeStruct((m, ncols), out_dtype)]
    args = [x, w]
    if rider is not None:
        r_in, r_out, r_shape = _rider_io(rider, lambda i, j: i * nj + j, (m // tm) * nj)
        in_specs.append(r_in)
        out_specs.append(r_out)
        out_shape.append(r_shape)
        args.append(rider.src)
    sem = ("parallel", "parallel") if rider is None else ("arbitrary", "arbitrary")
    outs = pl.pallas_call(
        kernel,
        grid=(m // tm, nj),
        in_specs=in_specs,
        out_specs=out_specs,
        out_shape=out_shape,
        compiler_params=_params(*sem),
        name=name,
    )(*args)
    return outs[0] if rider is None else tuple(outs)


def _proj_norm_kernel(x_ref, w_ref, g_ref, o_ref, *, gdim):
    acc = jnp.dot(x_ref[...], w_ref[...], preferred_element_type=F32)
    for c in range(acc.shape[1] // gdim):
        sl = slice(c * gdim, (c + 1) * gdim)
        o_ref[:, sl] = _group_rmsnorm(acc[:, sl], g_ref[...]).astype(o_ref.dtype)


def _proj_norm(x, w, g, col0, ncols, tm, tn, name):
    m, k = x.shape
    gdim = g.shape[0]
    j0 = col0 // tn
    return pl.pallas_call(
        functools.partial(_proj_norm_kernel, gdim=gdim),
        grid=(m // tm, ncols // tn),
        in_specs=[pl.BlockSpec((tm, k), lambda i, j: (i, 0)),
                  pl.BlockSpec((k, tn), lambda i, j: (0, j0 + j)),
                  pl.BlockSpec((1, gdim), lambda i, j: (0, 0))],
        out_specs=pl.BlockSpec((tm, tn), lambda i, j: (i, j)),
        out_shape=jax.ShapeDtypeStruct((m, ncols), BF16),
        compiler_params=_params("parallel", "parallel"),
        name=name,
    )(x, w, g.reshape(1, gdim))


def _proj_qk(x, w, g_cols, cos_t, sa_t, sb_t, seq, rope_dim, tm, tn, rider):
    m, k = x.shape
    ncols = g_cols.shape[0]
    nseq = seq // tm
    nj = ncols // tn
    half = rope_dim // 2
    rider_args = (rider.nblocks, rider.body)

    def kernel(x_ref, w_ref, g_ref, cos_ref, sa_ref, sb_ref, rsrc_ref, o_ref, rdst_ref):
        for r in range(tm // ROW_CHUNK):
            rows = slice(r * ROW_CHUNK, (r + 1) * ROW_CHUNK)
            acc = jnp.dot(x_ref[rows, :], w_ref[...], preferred_element_type=F32)
            for c in range(tn // LANES):
                sl = slice(c * LANES, (c + 1) * LANES)
                y = _group_rmsnorm(acc[:, sl], g_ref[:, sl])
                y = (y * cos_ref[rows, :] + pltpu.roll(y, LANES - half, 1) * sa_ref[rows, :]
                     + pltpu.roll(y, half, 1) * sb_ref[rows, :])
                o_ref[rows, sl] = y.astype(o_ref.dtype)
        _run_rider(*rider_args, pl.program_id(0) * nj + pl.program_id(1), rsrc_ref, rdst_ref)

    tab = pl.BlockSpec((tm, LANES), lambda i, j: (i % nseq, 0))
    r_in, r_out, r_shape = _rider_io(rider, lambda i, j: i * nj + j, (m // tm) * nj)
    return pl.pallas_call(
        kernel,
        grid=(m // tm, nj),
        in_specs=[pl.BlockSpec((tm, k), lambda i, j: (i, 0)),
                  pl.BlockSpec((k, tn), lambda i, j: (0, j)),
                  pl.BlockSpec((1, tn), lambda i, j: (0, j)),
                  tab, tab, tab, r_in],
        out_specs=[pl.BlockSpec((tm, tn), lambda i, j: (i, j)), r_out],
        out_shape=[jax.ShapeDtypeStruct((m, ncols), BF16), r_shape],
        compiler_params=_params("arbitrary", "arbitrary"),
        name="proj_qk",
    )(x, w, g_cols.reshape(1, ncols), cos_t, sa_t, sb_t, rider.src)


def _diff_attn_kernel(q_ref, k_ref, v_ref, lam_ref, gs_ref, rsrc_ref, o_ref, rdst_ref,
                      vt_scr, s_scr, p_scr, a_scr, m_scr, l_scr, acc_scr,
                      *, tq, tk, dk, lambda_init, rider_args):
    i = pl.program_id(2)
    c0 = dk ** -0.5 * LOG2E
    neg = -1e30
    step = (pl.program_id(0) * pl.num_programs(1) + pl.program_id(1)) * pl.num_programs(2) + i
    _run_rider(*rider_args, step, rsrc_ref, rdst_ref)

    @pl.when(i == 0)
    def _():
        for blk in range(vt_scr.shape[0]):
            vt_scr[blk] = v_ref[blk * tk:(blk + 1) * tk, :].T

    m_scr[...] = jnp.full(m_scr.shape, neg, F32)
    l_scr[...] = jnp.zeros(l_scr.shape, F32)
    acc_scr[...] = jnp.zeros(acc_scr.shape, F32)

    def scores(blk, slot, qlo=0):
        k = k_ref[pl.ds(pl.multiple_of(blk * tk, tk), tk), :]
        for c in range(2):
            sl = slice(c * dk, (c + 1) * dk)
            s_scr[slot, c, :, qlo:] = lax.dot_general(k[:, sl], q_ref[qlo:, sl], (((1,), (1,)), ((), ())),
                                                      preferred_element_type=F32)

    def softmax(slot, diag=None, qlo=0):
        if diag is not None:
            key_chunk = (diag * tk + lax.broadcasted_iota(jnp.int32, (tk, tq - qlo), 0)) // CHUNK
            q_chunk = (qlo + lax.broadcasted_iota(jnp.int32, (tk, tq - qlo), 1)) // CHUNK
            allowed = key_chunk <= q_chunk
        hk = tk // 4
        for c in range(2):
            s = s_scr[slot, c, :, qlo:]
            if diag is not None:
                s = jnp.where(allowed, s, neg)
            m_prev = m_scr[c, :, qlo:]
            m_new = jnp.maximum(m_prev, jnp.max(s, axis=0, keepdims=True))
            alpha = jnp.exp2((m_prev - m_new) * c0)
            l_new = alpha * l_scr[c, :, qlo:]
            for h in range(4):
                rows = slice(h * hk, (h + 1) * hk)
                sh = s_scr[slot, c, rows, qlo:]
                if diag is not None:
                    sh = jnp.where(allowed[rows], sh, neg)
                p = jnp.exp2((sh - m_new) * c0)
                l_new = l_new + jnp.sum(p, axis=0, keepdims=True)
                p_scr[slot, c, rows, qlo:] = p.astype(BF16)
            l_scr[c, :, qlo:] = l_new
            a_scr[slot, c, :, qlo:] = alpha
            m_scr[c, :, qlo:] = m_new

    def values(slot, blk, qlo=0):
        vt = vt_scr[blk]
        for c in range(2):
            acc_scr[c, :, qlo:] = (a_scr[slot, c, :, qlo:] * acc_scr[c, :, qlo:]
                                   + jnp.dot(vt, p_scr[slot, c, :, qlo:], preferred_element_type=F32))

    scores(2 * i + 1, 0, tk)
    scores(2 * i, 1)
    softmax(0, diag=1, qlo=tk)
    scores(0, 0)
    softmax(1, diag=0)
    values(0, 2 * i + 1, tk)

    def pair(j, last):
        scores(2 * j - 1, 1)
        softmax(0)
        values(1, jnp.where(j == 1, 2 * i, 2 * j - 3))
        if not last:
            scores(2 * j, 0)
        softmax(1)
        values(0, 2 * j - 2)

    def body(j, carry):
        pair(j, False)
        return carry

    lax.fori_loop(1, i, body, 0)

    @pl.when(i >= 1)
    def _():
        pair(i, True)

    values(1, jnp.where(i == 0, 0, 2 * i - 1))

    lam_v = lam_ref[...]
    lam = (jnp.exp(jnp.sum(lam_v[0:1] * lam_v[1:2], axis=-1, keepdims=True))
           - jnp.exp(jnp.sum(lam_v[2:3] * lam_v[3:4], axis=-1, keepdims=True)) + lambda_init)
    o = acc_scr[0] / l_scr[0] - lam * (acc_scr[1] / l_scr[1])
    ms = jnp.mean(o * o, axis=0, keepdims=True)
    y = o * lax.rsqrt(ms + EPS) * gs_ref[...] * (1.0 - lambda_init)
    o_ref[...] = y.T.astype(o_ref.dtype)


def _diff_attn(qk, v, lam_vecs, g_subln, batch, seq, heads, dk, lambda_init, tq, rider):
    dv = 2 * dk
    tk = tq // 2
    nq = seq // tq
    kernel = functools.partial(_diff_attn_kernel, tq=tq, tk=tk, dk=dk, lambda_init=lambda_init,
                               rider_args=(rider.nblocks, rider.body))
    r_in, r_out, r_shape = _rider_io(rider, lambda b, h, i: (b * heads + h) * nq + i, batch * heads * nq)
    return pl.pallas_call(
        kernel,
        grid=(batch, heads, nq),
        in_specs=[pl.BlockSpec((tq, dv), lambda b, h, i: (b * nq + i, h)),
                  pl.BlockSpec((seq, dv), lambda b, h, i: (b, heads + h)),
                  pl.BlockSpec((seq, dv), lambda b, h, i: (b, h)),
                  pl.BlockSpec((4, dk), lambda b, h, i: (0, 0)),
                  pl.BlockSpec((dv, 1), lambda b, h, i: (0, 0)), r_in],
        out_specs=[pl.BlockSpec((tq, dv), lambda b, h, i: (b * nq + i, h)), r_out],
        out_shape=[jax.ShapeDtypeStruct((batch * seq, heads * dv), BF16), r_shape],
        scratch_shapes=[pltpu.VMEM((seq // tk, dv, tk), BF16),
                        pltpu.VMEM((2, 2, tk, tq), F32), pltpu.VMEM((2, 2, tk, tq), BF16),
                        pltpu.VMEM((2, 2, 1, tq), F32), pltpu.VMEM((2, 1, tq), F32),
                        pltpu.VMEM((2, 1, tq), F32), pltpu.VMEM((2, dv, tq), F32)],
        compiler_params=_params("arbitrary", "arbitrary", "arbitrary"),
        name="diff_attn",
    )(qk, qk, v, lam_vecs, g_subln.reshape(dv, 1), rider.src)


def _pool_kernel(up_ref, prev_ref, wp_ref, ps_ref, o_ref, ext_scr, *, tm, nseq, halo, gdim):
    i = pl.program_id(0)
    ti = i % nseq
    ext_scr[halo:halo + tm, :] = up_ref[...]
    ext_scr[0:halo, :] = jnp.where(ti == 0, 0.0, prev_ref[...])
    t1 = (ti * tm + 1 + lax.broadcasted_iota(jnp.int32, (tm, 1), 0)).astype(F32)
    for g, win in enumerate(POOL_WINDOWS):
        sl = slice(g * gdim, (g + 1) * gdim)
        cur = ext_scr[halo:halo + tm, sl]
        tot = cur
        for d in range(1, win):
            tot = tot + ext_scr[halo - d:halo - d + tm, sl]
        y = tot / jnp.minimum(t1, float(win)) - cur
        z = jnp.dot(y.astype(BF16), wp_ref[g], preferred_element_type=F32) * ps_ref[g:g + 1, :]
        o_ref[:, sl] = z.astype(o_ref.dtype)


def _pool_mixer(up, w_pool, pool_scale, seq, tm):
    m, width = up.shape
    groups, gdim, _ = w_pool.shape
    halo = 2 * SUBLANES
    assert max(POOL_WINDOWS) <= halo
    nseq = seq // tm
    per = tm // halo
    kernel = functools.partial(_pool_kernel, tm=tm, nseq=nseq, halo=halo, gdim=gdim)
    return pl.pallas_call(
        kernel,
        grid=(m // tm,),
        in_specs=[pl.BlockSpec((tm, width), lambda i: (i, 0)),
                  pl.BlockSpec((halo, width), lambda i: (jnp.maximum(i * per - 1, 0), 0)),
                  pl.BlockSpec((groups, gdim, gdim), lambda i: (0, 0, 0)),
                  pl.BlockSpec((groups, gdim), lambda i: (0, 0))],
        out_specs=pl.BlockSpec((tm, width), lambda i: (i, 0)),
        out_shape=jax.ShapeDtypeStruct((m, width), BF16),
        scratch_shapes=[pltpu.VMEM((tm + halo, width), F32)],
        compiler_params=_params("parallel"),
        name="pool_mixer",
    )(up, up, w_pool, pool_scale)


def _mem_attn_kernel(q_ref, k_ref, v_ref, o_ref, *, heads, hd):
    scale = hd ** -0.5
    for h in range(heads):
        sl = slice(h * hd, (h + 1) * hd)
        s = lax.dot_general(q_ref[:, sl], k_ref[:, sl], (((1,), (1,)), ((), ())),
                            preferred_element_type=F32) * scale
        p = jnp.exp(s - jnp.max(s, axis=-1, keepdims=True))
        l = jnp.sum(p, axis=-1, keepdims=True)
        o = jnp.dot(p.astype(BF16), v_ref[:, sl], preferred_element_type=F32) / l
        o_ref[:, sl] = o.astype(o_ref.dtype)


def _mem_attn(qm, km, vm, batch, seq, mem_len, heads, tq):
    m, width = qm.shape
    nq = seq // tq
    kernel = functools.partial(_mem_attn_kernel, heads=heads, hd=width // heads)
    return pl.pallas_call(
        kernel,
        grid=(batch, nq),
        in_specs=[pl.BlockSpec((tq, width), lambda b, i: (b * nq + i, 0)),
                  pl.BlockSpec((mem_len, width), lambda b, i: (b, 0)),
                  pl.BlockSpec((mem_len, width), lambda b, i: (b, 0))],
        out_specs=pl.BlockSpec((tq, width), lambda b, i: (b * nq + i, 0)),
        out_shape=jax.ShapeDtypeStruct((m, width), BF16),
        compiler_params=_params("parallel", "parallel"),
        name="mem_attn",
    )(qm, km, vm)


def _merge_kernel(ya_ref, yb_ref, yc_ref, wa_ref, wb_ref, wc_ref, ga_ref, gb_ref, gc_ref, o_ref):
    out = jax.nn.sigmoid(ga_ref[...]) * jnp.dot(ya_ref[...], wa_ref[...], preferred_element_type=F32)
    out = out + jax.nn.sigmoid(gb_ref[...]) * jnp.dot(yb_ref[...], wb_ref[...],
                                                      preferred_element_type=F32)
    out = out + jax.nn.sigmoid(gc_ref[...]) * jnp.dot(yc_ref[...], wc_ref[...],
                                                      preferred_element_type=F32)
    o_ref[...] = out.astype(o_ref.dtype)


def _merge(ya, yb, yc, w_a, w_b, w_c, gl, tm, tn):
    m = ya.shape[0]
    d = w_a.shape[1]
    nj = d // tn
    row = lambda a: pl.BlockSpec((tm, a.shape[1]), lambda i, j: (i, 0))
    col = lambda w: pl.BlockSpec((w.shape[0], tn), lambda i, j: (0, j))
    gate = lambda b: pl.BlockSpec((tm, tn), lambda i, j: (i, b * nj + j))
    return pl.pallas_call(
        _merge_kernel,
        grid=(m // tm, nj),
        in_specs=[row(ya), row(yb), row(yc), col(w_a), col(w_b), col(w_c), gate(0), gate(1), gate(2)],
        out_specs=pl.BlockSpec((tm, tn), lambda i, j: (i, j)),
        out_shape=jax.ShapeDtypeStruct((m, d), BF16),
        compiler_params=_params("parallel", "parallel"),
        name="merge",
    )(ya, yb, yc, w_a, w_b, w_c, gl, gl, gl)


def _proj_residual_kernel(a_ref, w_ref, r_ref, o_ref):
    o_ref[...] = r_ref[...] + jnp.dot(a_ref[...], w_ref[...], preferred_element_type=F32)


def _proj_residual(a, w, res, tm, tn):
    m, k = a.shape
    n = w.shape[1]
    return pl.pallas_call(
        _proj_residual_kernel,
        grid=(m // tm, n // tn),
        in_specs=[pl.BlockSpec((tm, k), lambda i, j: (i, 0)),
                  pl.BlockSpec((k, tn), lambda i, j: (0, j)),
                  pl.BlockSpec((tm, tn), lambda i, j: (i, j))],
        out_specs=pl.BlockSpec((tm, tn), lambda i, j: (i, j)),
        out_shape=jax.ShapeDtypeStruct((m, n), F32),
        compiler_params=_params("parallel", "parallel"),
        name="proj_residual",
    )(a, w, res)


def _ffn_up_kernel(h_ref, wg_ref, wv_ref, cwg_ref, cwv_ref, cbg_ref, cbv_ref, o_ref, yg_scr, yv_scr,
                   *, tm, nseq):
    i = pl.program_id(1)
    pad = SUBLANES

    @pl.when(i % nseq == 0)
    def _():
        yg_scr[0:pad, :] = jnp.zeros((pad, yg_scr.shape[1]), F32)
        yv_scr[0:pad, :] = jnp.zeros((pad, yv_scr.shape[1]), F32)

    def conv(r, y_scr, w_ref, cw_ref, cb_ref):
        lo = pad + r * ROW_CHUNK
        y_scr[lo:lo + ROW_CHUNK, :] = jnp.dot(h_ref[r * ROW_CHUNK:(r + 1) * ROW_CHUNK, :], w_ref[...],
                                              preferred_element_type=F32)
        u = cb_ref[...] + y_scr[lo:lo + ROW_CHUNK, :] * cw_ref[CONV_W - 1:CONV_W, :]
        for d in range(1, CONV_W):
            u = u + y_scr[lo - d:lo - d + ROW_CHUNK, :] * cw_ref[CONV_W - 1 - d:CONV_W - d, :]
        return u

    for r in range(tm // ROW_CHUNK):
        gate = conv(r, yg_scr, wg_ref, cwg_ref, cbg_ref)
        val = conv(r, yv_scr, wv_ref, cwv_ref, cbv_ref)
        o_ref[r * ROW_CHUNK:(r + 1) * ROW_CHUNK, :] = (gate * jax.nn.sigmoid(gate) * val).astype(o_ref.dtype)
    yg_scr[0:pad, :] = yg_scr[tm:tm + pad, :]
    yv_scr[0:pad, :] = yv_scr[tm:tm + pad, :]


def _ffn_up(h, w_up, conv_w, conv_b, seq, tm, tn):
    m, k = h.shape
    ff = w_up.shape[1] // 2
    nj = ff // tn
    nseq = seq // tm
    kernel = functools.partial(_ffn_up_kernel, tm=tm, nseq=nseq)
    wspec = lambda half: pl.BlockSpec((k, tn), lambda j, i: (0, half * nj + j))
    cwspec = lambda half: pl.BlockSpec((CONV_W, tn), lambda j, i: (0, half * nj + j))
    cbspec = lambda half: pl.BlockSpec((1, tn), lambda j, i: (0, half * nj + j))
    scratch = pltpu.VMEM((tm + SUBLANES, tn), F32)
    return pl.pallas_call(
        kernel,
        grid=(nj, m // tm),
        in_specs=[pl.BlockSpec((tm, k), lambda j, i: (i, 0)),
                  wspec(0), wspec(1), cwspec(0), cwspec(1), cbspec(0), cbspec(1)],
        out_specs=pl.BlockSpec((tm, tn), lambda j, i: (i, j)),
        out_shape=jax.ShapeDtypeStruct((m, ff), BF16),
        scratch_shapes=[scratch, scratch],
        compiler_params=_params("arbitrary", "arbitrary"),
        name="ffn_up",
    )(h, w_up, w_up, conv_w, conv_w, conv_b.reshape(1, -1), conv_b.reshape(1, -1))


def _ffn_down_kernel(a_ref, w_ref, r_ref, o_ref, acc_scr, *, rem):
    kk = pl.program_id(2)
    last = pl.num_programs(2) - 1

    @pl.when(kk == 0)
    def _():
        acc_scr[...] = r_ref[...]

    @pl.when(kk < last)
    def _():
        acc_scr[...] += jnp.dot(a_ref[...], w_ref[...], preferred_element_type=F32)

    @pl.when(kk == last)
    def _():
        o_ref[...] = acc_scr[...] + jnp.dot(a_ref[:, :rem], w_ref[:rem, :], preferred_element_type=F32)


def _ffn_down(a, w, res, tm, tn, tk):
    m = a.shape[0]
    k, n = w.shape
    nk = pl.cdiv(k, tk)
    rem = k - (nk - 1) * tk
    assert tk % LANES == 0 and rem % LANES == 0 and a.shape[1] >= nk * tk
    return pl.pallas_call(
        functools.partial(_ffn_down_kernel, rem=rem),
        grid=(m // tm, n // tn, nk),
        in_specs=[pl.BlockSpec((tm, tk), lambda i, j, kk: (i, kk)),
                  pl.BlockSpec((tk, tn), lambda i, j, kk: (kk, j)),
                  pl.BlockSpec((tm, tn), lambda i, j, kk: (i, j))],
        out_specs=pl.BlockSpec((tm, tn), lambda i, j, kk: (i, j)),
        out_shape=jax.ShapeDtypeStruct((m, n), F32),
        scratch_shapes=[pltpu.VMEM((tm, tn), F32)],
        compiler_params=_params("parallel", "parallel", "arbitrary"),
        name="ffn_down",
    )(a, w, res)


def _rope_tables(seq, rope_dim):
    half = rope_dim // 2
    pos = jnp.arange(seq, dtype=F32)
    inv = ROPE_THETA ** (-jnp.arange(0, rope_dim, 2, dtype=F32) / rope_dim)
    ang = pos[:, None] * inv[None, :]
    cos, sin = jnp.cos(ang), jnp.sin(ang)
    ones = jnp.ones((seq, LANES - rope_dim), F32)
    zeros = jnp.zeros((seq, LANES - rope_dim), F32)
    zh = jnp.zeros((seq, half), F32)
    cos_t = jnp.concatenate([cos, cos, ones], axis=1)
    sa_t = jnp.concatenate([-sin, zh, zeros], axis=1)
    sb_t = jnp.concatenate([zh, sin, zeros], axis=1)
    return cos_t, sa_t, sb_t


def _pad_halves(a, ff, ff_pad):
    pad = [(0, 0)] * (a.ndim - 1) + [(0, ff_pad - ff)]
    return jnp.concatenate([jnp.pad(a[..., :ff], pad), jnp.pad(a[..., ff:], pad)], axis=-1)


def kernel(x, mem, g_attn_norm, w_in, g_qa, g_ka, lam_q1, lam_k1, lam_q2, lam_k2, g_subln, w_pool,
           pool_scale, g_mem, w_mkv, g_qm, g_km, w_a, w_b, w_c, w_o, g_ffn_norm, w_up, conv_w, conv_b,
           w_down):
    batch, seq, d_model = x.shape
    mem_len = mem.shape[1]
    depth = w_in.shape[0]
    dk = g_qa.shape[1]
    dv = g_subln.shape[1]
    heads = w_a.shape[1] // dv
    qk_width = heads * 2 * dk
    pool_width = w_b.shape[1]
    mem_width = w_c.shape[1]
    mem_hd = g_qm.shape[1]
    rope_dim = dk // 4
    d_ff = w_down.shape[1]
    ff_pad = -(-d_ff // 1024) * 1024
    assert dk == LANES and dv == 2 * dk

    m = batch * seq
    xf = x.reshape(m, d_model)
    cos_t, sa_t, sb_t = _rope_tables(seq, rope_dim)
    c_v = 2 * qk_width
    c_up = c_v + heads * dv
    c_qm = c_up + pool_width
    c_gl = c_qm + mem_width

    for l in range(depth):
        lambda_init = 0.8 - 0.6 * math.exp(-0.3 * l)
        g_qk = jnp.concatenate([jnp.tile(g_qa[l], qk_width // dk), jnp.tile(g_ka[l], qk_width // dk)])
        lam_vecs = jnp.stack([lam_q1[l], lam_k1[l], lam_q2[l], lam_k2[l]]).astype(F32)

        w_qk_b = w_in[l][:, :c_v].astype(BF16)
        nrest = (w_in.shape[2] - c_v) // c_v
        rest_rider = _Rider(w_in[l], (256, c_v), lambda n: (n // nrest, 1 + n % nrest),
                            (d_model, nrest * c_v), (256, c_v), lambda n: (n // nrest, n % nrest),
                            (d_model // 256) * nrest, _cast_body)

        h = _rmsnorm(xf, g_attn_norm[l], 256)
        qk, w_rest_b = _proj_qk(h, w_qk_b, g_qk, cos_t, sa_t, sb_t, seq, rope_dim, 1024, 1024, rest_rider)
        v = _proj_plain(h, w_rest_b, 0, heads * dv, BF16, 1024, 1024, "proj_v")
        up = _proj_plain(h, w_rest_b, c_up - c_v, pool_width, F32, 1024, 1024, "proj_up")
        qm = _proj_norm(h, w_rest_b, g_qm[l], c_qm - c_v, mem_width, 1024, 1024, "proj_qm")
        gl, w_down_b = _proj_plain(h, w_rest_b, c_gl - c_v, 3 * d_model, F32, 1024, 1024, "proj_gates",
                                   rider=_row_block_rider(w_down[l], 128, d_model, _cast_body))

        ya, w_up_b = _diff_attn(qk, v, lam_vecs, g_subln[l], batch, seq, heads, dk, lambda_init, 1024,
                                _row_block_rider(w_up[l], 32, 2 * ff_pad, _pad_halves_body))
        yb = _pool_mixer(up, w_pool[l].astype(BF16), pool_scale[l], seq, 512)

        mem_n = _rmsnorm(mem.reshape(batch * mem_len, d_model), g_mem[l], 256)
        w_mkv_b = w_mkv[l].astype(BF16)
        km = _proj_norm(mem_n, w_mkv_b, g_km[l], 0, mem_width, batch * mem_len, 1024, "proj_km")
        vm = _proj_plain(mem_n, w_mkv_b, mem_width, mem_width, BF16, batch * mem_len, 1024, "proj_vm")
        yc = _mem_attn(qm, km, vm, batch, seq, mem_len, mem_width // mem_hd, 512)

        merged = _merge(ya, yb, yc, w_a[l].astype(BF16), w_b[l].astype(BF16), w_c[l].astype(BF16),
                        gl, 1024, 512)
        xf = _proj_residual(merged, w_o[l].astype(BF16), xf, 1024, 512)

        h2 = _rmsnorm(xf, g_ffn_norm[l], 256)
        act = _ffn_up(h2, w_up_b, _pad_halves(conv_w[l], d_ff, ff_pad),
                      _pad_halves(conv_b[l], d_ff, ff_pad), seq, 1024, 512)
        xf = _ffn_down(act, w_down_b, xf, 1024, 1024, ff_pad // 4)

    return xf.reshape(batch, seq, d_model)
```

```python
import functools
import math
from typing import Callable, NamedTuple

import jax
import jax.numpy as jnp
from jax import lax
from jax.experimental import pallas as pl
from jax.experimental.pallas import tpu as pltpu

F32 = jnp.float32
BF16 = jnp.bfloat16

EPS = 1e-6
ROPE_THETA = 500000.0
CHUNK = 64
POOL_WINDOWS = (2, 4, 8, 16)
CONV_W = 3

LANES = 128
SUBLANES = 8
VMEM_LIMIT_BYTES = 56 * 1024 * 1024
LOG2E = 1.4426950408889634

ROW_TILE = 1024
COL_TILE = 1024
HALF_COL_TILE = 512
ROW_CHUNK = 256
NORM_ROWS = 512
ATTN_TILE = 1024
SIDE_TILE = 512
FF_ALIGN = 1024
FFN_DOWN_K_STEPS = 4
W_IN_RIDER_ROWS = 256
W_DOWN_RIDER_ROWS = 128
W_UP_RIDER_ROWS = 32


def _params(*sem):
    return pltpu.CompilerParams(dimension_semantics=sem, vmem_limit_bytes=VMEM_LIMIT_BYTES)


def _group_rmsnorm(x, g):
    ms = jnp.mean(x * x, axis=-1, keepdims=True)
    return x * lax.rsqrt(ms + EPS) * g


class _Rider(NamedTuple):
    src: jax.Array
    in_block: tuple
    in_index: Callable
    out_shape: tuple
    out_block: tuple
    out_index: Callable
    nblocks: int
    body: Callable


def _rider_io(rider, step_of, nsteps):
    assert rider.nblocks <= nsteps, "the host call has too few grid steps for this rider"
    block_of = lambda *g: jnp.minimum(step_of(*g), rider.nblocks - 1)
    in_spec = pl.BlockSpec(rider.in_block, lambda *g: rider.in_index(block_of(*g)))
    out_spec = pl.BlockSpec(rider.out_block, lambda *g: rider.out_index(block_of(*g)))
    return in_spec, out_spec, jax.ShapeDtypeStruct(rider.out_shape, BF16)


def _run_rider(nblocks, body, step, src_ref, dst_ref):
    @pl.when(step < nblocks)
    def _():
        body(src_ref, dst_ref)


def _cast_body(src_ref, dst_ref):
    dst_ref[...] = src_ref[...].astype(dst_ref.dtype)


def _pad_halves_body(src_ref, dst_ref):
    ff = src_ref.shape[1] // 2
    ff_pad = dst_ref.shape[1] // 2
    zeros = jnp.zeros((dst_ref.shape[0], ff_pad - ff), dst_ref.dtype)
    for half in range(2):
        dst_ref[:, half * ff_pad:half * ff_pad + ff] = src_ref[:, half * ff:(half + 1) * ff].astype(dst_ref.dtype)
        dst_ref[:, half * ff_pad + ff:(half + 1) * ff_pad] = zeros


def _row_block_rider(src, tr, out_cols, body):
    rows, cols = src.shape
    assert rows % tr == 0
    return _Rider(src, (tr, cols), lambda n: (n, 0), (rows, out_cols), (tr, out_cols), lambda n: (n, 0),
                  rows // tr, body)


def _rmsnorm_kernel(x_ref, g_ref, o_ref):
    o_ref[...] = _group_rmsnorm(x_ref[...], g_ref[...]).astype(o_ref.dtype)


def _rmsnorm(x, g, tm):
    m, d = x.shape
    return pl.pallas_call(
        _rmsnorm_kernel,
        grid=(m // tm,),
        in_specs=[pl.BlockSpec((tm, d), lambda i: (i, 0)), pl.BlockSpec((1, d), lambda i: (0, 0))],
        out_specs=pl.BlockSpec((tm, d), lambda i: (i, 0)),
        out_shape=jax.ShapeDtypeStruct((m, d), BF16),
        compiler_params=_params("parallel"),
        name="rmsnorm",
    )(x, g.reshape(1, d))


def _proj_plain(x, w, col0, ncols, out_dtype, tm, tn, name, rider=None):
    m, k = x.shape
    j0 = col0 // tn
    nj = ncols // tn
    rider_args = None if rider is None else (rider.nblocks, rider.body)

    def kernel(x_ref, w_ref, *refs):
        o_ref = refs[-1] if rider_args is None else refs[1]
        o_ref[...] = jnp.dot(x_ref[...], w_ref[...], preferred_element_type=F32).astype(o_ref.dtype)
        if rider_args is not None:
            _run_rider(*rider_args, pl.program_id(0) * nj + pl.program_id(1), refs[0], refs[2])

    in_specs = [pl.BlockSpec((tm, k), lambda i, j: (i, 0)),
                pl.BlockSpec((k, tn), lambda i, j: (0, j0 + j))]
    out_specs = [pl.BlockSpec((tm, tn), lambda i, j: (i, j))]
    out_shape = [jax.ShapeDtypeStruct((m, ncols), out_dtype)]
    args = [x, w]
    if rider is not None:
        r_in, r_out, r_shape = _rider_io(rider, lambda i, j: i * nj + j, (m // tm) * nj)
        in_specs.append(r_in)
        out_specs.append(r_out)
        out_shape.append(r_shape)
        args.append(rider.src)
    sem = ("parallel", "parallel") if rider is None else ("arbitrary", "arbitrary")
    outs = pl.pallas_call(
        kernel,
        grid=(m // tm, nj),
        in_specs=in_specs,
        out_specs=out_specs,
        out_shape=out_shape,
        compiler_params=_params(*sem),
        name=name,
    )(*args)
    return outs[0] if rider is None else tuple(outs)


def _proj_norm_kernel(x_ref, w_ref, g_ref, o_ref, *, gdim):
    acc = jnp.dot(x_ref[...], w_ref[...], preferred_element_type=F32)
    for c in range(acc.shape[1] // gdim):
        sl = slice(c * gdim, (c + 1) * gdim)
        o_ref[:, sl] = _group_rmsnorm(acc[:, sl], g_ref[...]).astype(o_ref.dtype)


def _proj_norm(x, w, g, col0, ncols, tm, tn, name):
    m, k = x.shape
    gdim = g.shape[0]
    j0 = col0 // tn
    return pl.pallas_call(
        functools.partial(_proj_norm_kernel, gdim=gdim),
        grid=(m // tm, ncols // tn),
        in_specs=[pl.BlockSpec((tm, k), lambda i, j: (i, 0)),
                  pl.BlockSpec((k, tn), lambda i, j: (0, j0 + j)),
                  pl.BlockSpec((1, gdim), lambda i, j: (0, 0))],
        out_specs=pl.BlockSpec((tm, tn), lambda i, j: (i, j)),
        out_shape=jax.ShapeDtypeStruct((m, ncols), BF16),
        compiler_params=_params("parallel", "parallel"),
        name=name,
    )(x, w, g.reshape(1, gdim))


def _proj_qk(x, w, g_cols, cos_t, sa_t, sb_t, seq, rope_dim, tm, tn, rider):
    m, k = x.shape
    ncols = g_cols.shape[0]
    nseq = seq // tm
    nj = ncols // tn
    half = rope_dim // 2
    rider_args = (rider.nblocks, rider.body)

    def kernel(x_ref, w_ref, g_ref, cos_ref, sa_ref, sb_ref, rsrc_ref, o_ref, rdst_ref):
        for r in range(tm // ROW_CHUNK):
            rows = slice(r * ROW_CHUNK, (r + 1) * ROW_CHUNK)
            acc = jnp.dot(x_ref[rows, :], w_ref[...], preferred_element_type=F32)
            for c in range(tn // LANES):
                sl = slice(c * LANES, (c + 1) * LANES)
                y = _group_rmsnorm(acc[:, sl], g_ref[:, sl])
                y = (y * cos_ref[rows, :] + pltpu.roll(y, LANES - half, 1) * sa_ref[rows, :]
                     + pltpu.roll(y, half, 1) * sb_ref[rows, :])
                o_ref[rows, sl] = y.astype(o_ref.dtype)
        _run_rider(*rider_args, pl.program_id(0) * nj + pl.program_id(1), rsrc_ref, rdst_ref)

    tab = pl.BlockSpec((tm, LANES), lambda i, j: (i % nseq, 0))
    r_in, r_out, r_shape = _rider_io(rider, lambda i, j: i * nj + j, (m // tm) * nj)
    return pl.pallas_call(
        kernel,
        grid=(m // tm, nj),
        in_specs=[pl.BlockSpec((tm, k), lambda i, j: (i, 0)),
                  pl.BlockSpec((k, tn), lambda i, j: (0, j)),
                  pl.BlockSpec((1, tn), lambda i, j: (0, j)),
                  tab, tab, tab, r_in],
        out_specs=[pl.BlockSpec((tm, tn), lambda i, j: (i, j)), r_out],
        out_shape=[jax.ShapeDtypeStruct((m, ncols), BF16), r_shape],
        compiler_params=_params("arbitrary", "arbitrary"),
        name="proj_qk",
    )(x, w, g_cols.reshape(1, ncols), cos_t, sa_t, sb_t, rider.src)


def _diff_attn_kernel(q_ref, k_ref, v_ref, lam_ref, gs_ref, rsrc_ref, o_ref, rdst_ref,
                      vt_scr, s_scr, p_scr, a_scr, m_scr, l_scr, acc_scr,
                      *, tq, tk, dk, lambda_init, rider_args):
    i = pl.program_id(2)
    c0 = dk ** -0.5 * LOG2E
    neg = -1e30
    step = (pl.program_id(0) * pl.num_programs(1) + pl.program_id(1)) * pl.num_programs(2) + i
    _run_rider(*rider_args, step, rsrc_ref, rdst_ref)

    @pl.when(i == 0)
    def _():
        for blk in range(vt_scr.shape[0]):
            vt_scr[blk] = v_ref[blk * tk:(blk + 1) * tk, :].T

    m_scr[...] = jnp.full(m_scr.shape, neg, F32)
    l_scr[...] = jnp.zeros(l_scr.shape, F32)
    acc_scr[...] = jnp.zeros(acc_scr.shape, F32)

    def scores(blk, slot, qlo=0):
        k = k_ref[pl.ds(pl.multiple_of(blk * tk, tk), tk), :]
        for c in range(2):
            sl = slice(c * dk, (c + 1) * dk)
            s_scr[slot, c, :, qlo:] = lax.dot_general(k[:, sl], q_ref[qlo:, sl], (((1,), (1,)), ((), ())),
                                                      preferred_element_type=F32)

    def softmax(slot, diag=None, qlo=0):
        if diag is not None:
            key_chunk = (diag * tk + lax.broadcasted_iota(jnp.int32, (tk, tq - qlo), 0)) // CHUNK
            q_chunk = (qlo + lax.broadcasted_iota(jnp.int32, (tk, tq - qlo), 1)) // CHUNK
            allowed = key_chunk <= q_chunk
        hk = tk // 4
        for c in range(2):
            s = s_scr[slot, c, :, qlo:]
            if diag is not None:
                s = jnp.where(allowed, s, neg)
            m_prev = m_scr[c, :, qlo:]
            m_new = jnp.maximum(m_prev, jnp.max(s, axis=0, keepdims=True))
            alpha = jnp.exp2((m_prev - m_new) * c0)
            l_new = alpha * l_scr[c, :, qlo:]
            for h in range(4):
                rows = slice(h * hk, (h + 1) * hk)
                sh = s_scr[slot, c, rows, qlo:]
                if diag is not None:
                    sh = jnp.where(allowed[rows], sh, neg)
                p = jnp.exp2((sh - m_new) * c0)
                l_new = l_new + jnp.sum(p, axis=0, keepdims=True)
                p_scr[slot, c, rows, qlo:] = p.astype(BF16)
            l_scr[c, :, qlo:] = l_new
            a_scr[slot, c, :, qlo:] = alpha
            m_scr[c, :, qlo:] = m_new

    def values(slot, blk, qlo=0):
        vt = vt_scr[blk]
        for c in range(2):
            acc_scr[c, :, qlo:] = (a_scr[slot, c, :, qlo:] * acc_scr[c, :, qlo:]
                                   + jnp.dot(vt, p_scr[slot, c, :, qlo:], preferred_element_type=F32))

    scores(2 * i + 1, 0, tk)
    scores(2 * i, 1)
    softmax(0, diag=1, qlo=tk)
    scores(0, 0)
    softmax(1, diag=0)
    values(0, 2 * i + 1, tk)

    def pair(j, last):
        scores(2 * j - 1, 1)
        softmax(0)
        values(1, jnp.where(j == 1, 2 * i, 2 * j - 3))
        if not last:
            scores(2 * j, 0)
        softmax(1)
        values(0, 2 * j - 2)

    def body(j, carry):
        pair(j, False)
        return carry

    lax.fori_loop(1, i, body, 0)

    @pl.when(i >= 1)
    def _():
        pair(i, True)

    values(1, jnp.where(i == 0, 0, 2 * i - 1))

    lam_v = lam_ref[...]
    lam = (jnp.exp(jnp.sum(lam_v[0:1] * lam_v[1:2], axis=-1, keepdims=True))
           - jnp.exp(jnp.sum(lam_v[2:3] * lam_v[3:4], axis=-1, keepdims=True)) + lambda_init)
    o = acc_scr[0] / l_scr[0] - lam * (acc_scr[1] / l_scr[1])
    ms = jnp.mean(o * o, axis=0, keepdims=True)
    y = o * lax.rsqrt(ms + EPS) * gs_ref[...] * (1.0 - lambda_init)
    o_ref[...] = y.T.astype(o_ref.dtype)


def _diff_attn(qk, v, lam_vecs, g_subln, batch, seq, heads, dk, lambda_init, tq, rider):
    dv = 2 * dk
    tk = tq // 2
    nq = seq // tq
    kernel = functools.partial(_diff_attn_kernel, tq=tq, tk=tk, dk=dk, lambda_init=lambda_init,
                               rider_args=(rider.nblocks, rider.body))
    r_in, r_out, r_shape = _rider_io(rider, lambda b, h, i: (b * heads + h) * nq + i, batch * heads * nq)
    return pl.pallas_call(
        kernel,
        grid=(batch, heads, nq),
        in_specs=[pl.BlockSpec((tq, dv), lambda b, h, i: (b * nq + i, h)),
                  pl.BlockSpec((seq, dv), lambda b, h, i: (b, heads + h)),
                  pl.BlockSpec((seq, dv), lambda b, h, i: (b, h)),
                  pl.BlockSpec((4, dk), lambda b, h, i: (0, 0)),
                  pl.BlockSpec((dv, 1), lambda b, h, i: (0, 0)), r_in],
        out_specs=[pl.BlockSpec((tq, dv), lambda b, h, i: (b * nq + i, h)), r_out],
        out_shape=[jax.ShapeDtypeStruct((batch * seq, heads * dv), BF16), r_shape],
        scratch_shapes=[pltpu.VMEM((seq // tk, dv, tk), BF16),
                        pltpu.VMEM((2, 2, tk, tq), F32), pltpu.VMEM((2, 2, tk, tq), BF16),
                        pltpu.VMEM((2, 2, 1, tq), F32), pltpu.VMEM((2, 1, tq), F32),
                        pltpu.VMEM((2, 1, tq), F32), pltpu.VMEM((2, dv, tq), F32)],
        compiler_params=_params("arbitrary", "arbitrary", "arbitrary"),
        name="diff_attn",
    )(qk, qk, v, lam_vecs, g_subln.reshape(dv, 1), rider.src)


def _pool_kernel(up_ref, prev_ref, wp_ref, ps_ref, o_ref, ext_scr, *, tm, nseq, halo, gdim):
    i = pl.program_id(0)
    ti = i % nseq
    ext_scr[halo:halo + tm, :] = up_ref[...]
    ext_scr[0:halo, :] = jnp.where(ti == 0, 0.0, prev_ref[...])
    t1 = (ti * tm + 1 + lax.broadcasted_iota(jnp.int32, (tm, 1), 0)).astype(F32)
    for g, win in enumerate(POOL_WINDOWS):
        sl = slice(g * gdim, (g + 1) * gdim)
        cur = ext_scr[halo:halo + tm, sl]
        tot = cur
        for d in range(1, win):
            tot = tot + ext_scr[halo - d:halo - d + tm, sl]
        y = tot / jnp.minimum(t1, float(win)) - cur
        z = jnp.dot(y.astype(BF16), wp_ref[g], preferred_element_type=F32) * ps_ref[g:g + 1, :]
        o_ref[:, sl] = z.astype(o_ref.dtype)


def _pool_mixer(up, w_pool, pool_scale, seq, tm):
    m, width = up.shape
    groups, gdim, _ = w_pool.shape
    halo = 2 * SUBLANES
    assert max(POOL_WINDOWS) <= halo
    nseq = seq // tm
    per = tm // halo
    kernel = functools.partial(_pool_kernel, tm=tm, nseq=nseq, halo=halo, gdim=gdim)
    return pl.pallas_call(
        kernel,
        grid=(m // tm,),
        in_specs=[pl.BlockSpec((tm, width), lambda i: (i, 0)),
                  pl.BlockSpec((halo, width), lambda i: (jnp.maximum(i * per - 1, 0), 0)),
                  pl.BlockSpec((groups, gdim, gdim), lambda i: (0, 0, 0)),
                  pl.BlockSpec((groups, gdim), lambda i: (0, 0))],
        out_specs=pl.BlockSpec((tm, width), lambda i: (i, 0)),
        out_shape=jax.ShapeDtypeStruct((m, width), BF16),
        scratch_shapes=[pltpu.VMEM((tm + halo, width), F32)],
        compiler_params=_params("parallel"),
        name="pool_mixer",
    )(up, up, w_pool, pool_scale)


def _mem_attn_kernel(q_ref, k_ref, v_ref, o_ref, *, heads, hd):
    scale = hd ** -0.5
    for h in range(heads):
        sl = slice(h * hd, (h + 1) * hd)
        s = lax.dot_general(q_ref[:, sl], k_ref[:, sl], (((1,), (1,)), ((), ())),
                            preferred_element_type=F32) * scale
        p = jnp.exp(s - jnp.max(s, axis=-1, keepdims=True))
        l = jnp.sum(p, axis=-1, keepdims=True)
        o = jnp.dot(p.astype(BF16), v_ref[:, sl], preferred_element_type=F32) / l
        o_ref[:, sl] = o.astype(o_ref.dtype)


def _mem_attn(qm, km, vm, batch, seq, mem_len, heads, tq):
    m, width = qm.shape
    nq = seq // tq
    kernel = functools.partial(_mem_attn_kernel, heads=heads, hd=width // heads)
    return pl.pallas_call(
        kernel,
        grid=(batch, nq),
        in_specs=[pl.BlockSpec((tq, width), lambda b, i: (b * nq + i, 0)),
                  pl.BlockSpec((mem_len, width), lambda b, i: (b, 0)),
                  pl.BlockSpec((mem_len, width), lambda b, i: (b, 0))],
        out_specs=pl.BlockSpec((tq, width), lambda b, i: (b * nq + i, 0)),
        out_shape=jax.ShapeDtypeStruct((m, width), BF16),
        compiler_params=_params("parallel", "parallel"),
        name="mem_attn",
    )(qm, km, vm)


def _merge_kernel(ya_ref, yb_ref, yc_ref, wa_ref, wb_ref, wc_ref, ga_ref, gb_ref, gc_ref, o_ref):
    out = jax.nn.sigmoid(ga_ref[...]) * jnp.dot(ya_ref[...], wa_ref[...], preferred_element_type=F32)
    out = out + jax.nn.sigmoid(gb_ref[...]) * jnp.dot(yb_ref[...], wb_ref[...],
                                                      preferred_element_type=F32)
    out = out + jax.nn.sigmoid(gc_ref[...]) * jnp.dot(yc_ref[...], wc_ref[...],
                                                      preferred_element_type=F32)
    o_ref[...] = out.astype(o_ref.dtype)


def _merge(ya, yb, yc, w_a, w_b, w_c, gl, tm, tn):
    m = ya.shape[0]
    d = w_a.shape[1]
    nj = d // tn
    row = lambda a: pl.BlockSpec((tm, a.shape[1]), lambda i, j: (i, 0))
    col = lambda w: pl.BlockSpec((w.shape[0], tn), lambda i, j: (0, j))
    gate = lambda b: pl.BlockSpec((tm, tn), lambda i, j: (i, b * nj + j))
    return pl.pallas_call(
        _merge_kernel,
        grid=(m // tm, nj),
        in_specs=[row(ya), row(yb), row(yc), col(w_a), col(w_b), col(w_c), gate(0), gate(1), gate(2)],
        out_specs=pl.BlockSpec((tm, tn), lambda i, j: (i, j)),
        out_shape=jax.ShapeDtypeStruct((m, d), BF16),
        compiler_params=_params("parallel", "parallel"),
        name="merge",
    )(ya, yb, yc, w_a, w_b, w_c, gl, gl, gl)


def _proj_residual_kernel(a_ref, w_ref, r_ref, o_ref):
    o_ref[...] = r_ref[...] + jnp.dot(a_ref[...], w_ref[...], preferred_element_type=F32)


def _proj_residual(a, w, res, tm, tn):
    m, k = a.shape
    n = w.shape[1]
    return pl.pallas_call(
        _proj_residual_kernel,
        grid=(m // tm, n // tn),
        in_specs=[pl.BlockSpec((tm, k), lambda i, j: (i, 0)),
                  pl.BlockSpec((k, tn), lambda i, j: (0, j)),
                  pl.BlockSpec((tm, tn), lambda i, j: (i, j))],
        out_specs=pl.BlockSpec((tm, tn), lambda i, j: (i, j)),
        out_shape=jax.ShapeDtypeStruct((m, n), F32),
        compiler_params=_params("parallel", "parallel"),
        name="proj_residual",
    )(a, w, res)


def _ffn_up_kernel(h_ref, wg_ref, wv_ref, cwg_ref, cwv_ref, cbg_ref, cbv_ref, o_ref, yg_scr, yv_scr,
                   *, tm, nseq):
    i = pl.program_id(1)
    pad = SUBLANES

    @pl.when(i % nseq == 0)
    def _():
        yg_scr[0:pad, :] = jnp.zeros((pad, yg_scr.shape[1]), F32)
        yv_scr[0:pad, :] = jnp.zeros((pad, yv_scr.shape[1]), F32)

    def conv(r, y_scr, w_ref, cw_ref, cb_ref):
        lo = pad + r * ROW_CHUNK
        y_scr[lo:lo + ROW_CHUNK, :] = jnp.dot(h_ref[r * ROW_CHUNK:(r + 1) * ROW_CHUNK, :], w_ref[...],
                                              preferred_element_type=F32)
        u = cb_ref[...] + y_scr[lo:lo + ROW_CHUNK, :] * cw_ref[CONV_W - 1:CONV_W, :]
        for d in range(1, CONV_W):
            u = u + y_scr[lo - d:lo - d + ROW_CHUNK, :] * cw_ref[CONV_W - 1 - d:CONV_W - d, :]
        return u

    for r in range(tm // ROW_CHUNK):
        gate = conv(r, yg_scr, wg_ref, cwg_ref, cbg_ref)
        val = conv(r, yv_scr, wv_ref, cwv_ref, cbv_ref)
        o_ref[r * ROW_CHUNK:(r + 1) * ROW_CHUNK, :] = (gate * jax.nn.sigmoid(gate) * val).astype(o_ref.dtype)
    yg_scr[0:pad, :] = yg_scr[tm:tm + pad, :]
    yv_scr[0:pad, :] = yv_scr[tm:tm + pad, :]


def _ffn_up(h, w_up, conv_w, conv_b, seq, tm, tn):
    m, k = h.shape
    ff = w_up.shape[1] // 2
    nj = ff // tn
    nseq = seq // tm
    kernel = functools.partial(_ffn_up_kernel, tm=tm, nseq=nseq)
    wspec = lambda half: pl.BlockSpec((k, tn), lambda j, i: (0, half * nj + j))
    cwspec = lambda half: pl.BlockSpec((CONV_W, tn), lambda j, i: (0, half * nj + j))
    cbspec = lambda half: pl.BlockSpec((1, tn), lambda j, i: (0, half * nj + j))
    scratch = pltpu.VMEM((tm + SUBLANES, tn), F32)
    return pl.pallas_call(
        kernel,
        grid=(nj, m // tm),
        in_specs=[pl.BlockSpec((tm, k), lambda j, i: (i, 0)),
                  wspec(0), wspec(1), cwspec(0), cwspec(1), cbspec(0), cbspec(1)],
        out_specs=pl.BlockSpec((tm, tn), lambda j, i: (i, j)),
        out_shape=jax.ShapeDtypeStruct((m, ff), BF16),
        scratch_shapes=[scratch, scratch],
        compiler_params=_params("arbitrary", "arbitrary"),
        name="ffn_up",
    )(h, w_up, w_up, conv_w, conv_w, conv_b.reshape(1, -1), conv_b.reshape(1, -1))


def _ffn_down_kernel(a_ref, w_ref, r_ref, o_ref, acc_scr, *, rem):
    kk = pl.program_id(2)
    last = pl.num_programs(2) - 1

    @pl.when(kk == 0)
    def _():
        acc_scr[...] = r_ref[...]

    @pl.when(kk < last)
    def _():
        acc_scr[...] += jnp.dot(a_ref[...], w_ref[...], preferred_element_type=F32)

    @pl.when(kk == last)
    def _():
        o_ref[...] = acc_scr[...] + jnp.dot(a_ref[:, :rem], w_ref[:rem, :], preferred_element_type=F32)


def _ffn_down(a, w, res, tm, tn, tk):
    m = a.shape[0]
    k, n = w.shape
    nk = pl.cdiv(k, tk)
    rem = k - (nk - 1) * tk
    assert tk % LANES == 0 and rem % LANES == 0 and a.shape[1] >= nk * tk
    return pl.pallas_call(
        functools.partial(_ffn_down_kernel, rem=rem),
        grid=(m // tm, n // tn, nk),
        in_specs=[pl.BlockSpec((tm, tk), lambda i, j, kk: (i, kk)),
                  pl.BlockSpec((tk, tn), lambda i, j, kk: (kk, j)),
                  pl.BlockSpec((tm, tn), lambda i, j, kk: (i, j))],
        out_specs=pl.BlockSpec((tm, tn), lambda i, j, kk: (i, j)),
        out_shape=jax.ShapeDtypeStruct((m, n), F32),
        scratch_shapes=[pltpu.VMEM((tm, tn), F32)],
        compiler_params=_params("parallel", "parallel", "arbitrary"),
        name="ffn_down",
    )(a, w, res)


def _rope_tables(seq, rope_dim):
    half = rope_dim // 2
    pos = jnp.arange(seq, dtype=F32)
    inv = ROPE_THETA ** (-jnp.arange(0, rope_dim, 2, dtype=F32) / rope_dim)
    ang = pos[:, None] * inv[None, :]
    cos, sin = jnp.cos(ang), jnp.sin(ang)
    ones = jnp.ones((seq, LANES - rope_dim), F32)
    zeros = jnp.zeros((seq, LANES - rope_dim), F32)
    zh = jnp.zeros((seq, half), F32)
    cos_t = jnp.concatenate([cos, cos, ones], axis=1)
    sa_t = jnp.concatenate([-sin, zh, zeros], axis=1)
    sb_t = jnp.concatenate([zh, sin, zeros], axis=1)
    return cos_t, sa_t, sb_t


def _pad_halves(a, ff, ff_pad):
    pad = [(0, 0)] * (a.ndim - 1) + [(0, ff_pad - ff)]
    return jnp.concatenate([jnp.pad(a[..., :ff], pad), jnp.pad(a[..., ff:], pad)], axis=-1)


def kernel(x, mem, g_attn_norm, w_in, g_qa, g_ka, lam_q1, lam_k1, lam_q2, lam_k2, g_subln, w_pool,
           pool_scale, g_mem, w_mkv, g_qm, g_km, w_a, w_b, w_c, w_o, g_ffn_norm, w_up, conv_w, conv_b,
           w_down):
    batch, seq, d_model = x.shape
    mem_len = mem.shape[1]
    depth = w_in.shape[0]
    dk = g_qa.shape[1]
    dv = g_subln.shape[1]
    heads = w_a.shape[1] // dv
    qk_width = heads * 2 * dk
    pool_width = w_b.shape[1]
    mem_width = w_c.shape[1]
    mem_hd = g_qm.shape[1]
    rope_dim = dk // 4
    d_ff = w_down.shape[1]
    ff_pad = -(-d_ff // FF_ALIGN) * FF_ALIGN
    assert dk == LANES and dv == 2 * dk
    tm, tn, tnh = ROW_TILE, COL_TILE, HALF_COL_TILE

    m = batch * seq
    xf = x.reshape(m, d_model)
    cos_t, sa_t, sb_t = _rope_tables(seq, rope_dim)
    c_v = 2 * qk_width
    c_up = c_v + heads * dv
    c_qm = c_up + pool_width
    c_gl = c_qm + mem_width

    for l in range(depth):
        lambda_init = 0.8 - 0.6 * math.exp(-0.3 * l)
        g_qk = jnp.concatenate([jnp.tile(g_qa[l], qk_width // dk), jnp.tile(g_ka[l], qk_width // dk)])
        lam_vecs = jnp.stack([lam_q1[l], lam_k1[l], lam_q2[l], lam_k2[l]]).astype(F32)

        w_qk_b = w_in[l][:, :c_v].astype(BF16)
        nrest = (w_in.shape[2] - c_v) // c_v
        rr = W_IN_RIDER_ROWS
        rest_rider = _Rider(w_in[l], (rr, c_v), lambda n: (n // nrest, 1 + n % nrest),
                            (d_model, nrest * c_v), (rr, c_v), lambda n: (n // nrest, n % nrest),
                            (d_model // rr) * nrest, _cast_body)

        h = _rmsnorm(xf, g_attn_norm[l], NORM_ROWS)
        qk, w_rest_b = _proj_qk(h, w_qk_b, g_qk, cos_t, sa_t, sb_t, seq, rope_dim, tm, tn, rest_rider)
        v = _proj_plain(h, w_rest_b, 0, heads * dv, BF16, tm, tn, "proj_v")
        up = _proj_plain(h, w_rest_b, c_up - c_v, pool_width, F32, tm, tn, "proj_up")
        qm = _proj_norm(h, w_rest_b, g_qm[l], c_qm - c_v, mem_width, tm, tn, "proj_qm")
        gl, w_down_b = _proj_plain(h, w_rest_b, c_gl - c_v, 3 * d_model, F32, tm, tn, "proj_gates",
                                   rider=_row_block_rider(w_down[l], W_DOWN_RIDER_ROWS, d_model, _cast_body))

        ya, w_up_b = _diff_attn(qk, v, lam_vecs, g_subln[l], batch, seq, heads, dk, lambda_init, ATTN_TILE,
                                _row_block_rider(w_up[l], W_UP_RIDER_ROWS, 2 * ff_pad, _pad_halves_body))
        yb = _pool_mixer(up, w_pool[l].astype(BF16), pool_scale[l], seq, SIDE_TILE)

        mem_rows = batch * mem_len
        mem_n = _rmsnorm(mem.reshape(mem_rows, d_model), g_mem[l], NORM_ROWS)
        w_mkv_b = w_mkv[l].astype(BF16)
        km = _proj_norm(mem_n, w_mkv_b, g_km[l], 0, mem_width, mem_rows, tn, "proj_km")
        vm = _proj_plain(mem_n, w_mkv_b, mem_width, mem_width, BF16, mem_rows, tn, "proj_vm")
        yc = _mem_attn(qm, km, vm, batch, seq, mem_len, mem_width // mem_hd, SIDE_TILE)

        merged = _merge(ya, yb, yc, w_a[l].astype(BF16), w_b[l].astype(BF16), w_c[l].astype(BF16),
                        gl, tm, tnh)
        xf = _proj_residual(merged, w_o[l].astype(BF16), xf, tm, tnh)

        h2 = _rmsnorm(xf, g_ffn_norm[l], NORM_ROWS)
        act = _ffn_up(h2, w_up_b, _pad_halves(conv_w[l], d_ff, ff_pad),
                      _pad_halves(conv_b[l], d_ff, ff_pad), seq, tm, tnh)
        xf = _ffn_down(act, w_down_b, xf, tm, tn, ff_pad // FFN_DOWN_K_STEPS)

    return xf.reshape(batch, seq, d_model)
```

```python
import functools
import math
from typing import Callable, NamedTuple

import jax
import jax.numpy as jnp
from jax import lax
from jax.experimental import pallas as pl
from jax.experimental.pallas import tpu as pltpu

F32 = jnp.float32
BF16 = jnp.bfloat16

EPS = 1e-6
ROPE_THETA = 500000.0
CHUNK = 64
POOL_WINDOWS = (2, 4, 8, 16)
CONV_W = 3

LANES = 128
SUBLANES = 8
SUM_ROWS = 2 * SUBLANES
VMEM_LIMIT_BYTES = 56 * 1024 * 1024
LOG2E = 1.4426950408889634

ROW_TILE = 1024
COL_TILE = 1024
HALF_COL_TILE = 512
ROW_CHUNK = 256
NORM_ROWS = 512
ATTN_TILE = 1024
SIDE_TILE = 512
FF_ALIGN = 1024
FFN_DOWN_K_STEPS = 4
W_IN_RIDER_ROWS = 256
W_DOWN_RIDER_ROWS = 128
W_UP_RIDER_ROWS = 32


def _params(*sem):
    return pltpu.CompilerParams(dimension_semantics=sem, vmem_limit_bytes=VMEM_LIMIT_BYTES)


def _group_rmsnorm(x, g):
    ms = jnp.mean(x * x, axis=-1, keepdims=True)
    return x * lax.rsqrt(ms + EPS) * g


class _Rider(NamedTuple):
    src: jax.Array
    in_block: tuple
    in_index: Callable
    out_shape: tuple
    out_block: tuple
    out_index: Callable
    nblocks: int
    body: Callable


def _rider_io(rider, step_of, nsteps):
    assert rider.nblocks <= nsteps, "the host call has too few grid steps for this rider"
    block_of = lambda *g: jnp.minimum(step_of(*g), rider.nblocks - 1)
    in_spec = pl.BlockSpec(rider.in_block, lambda *g: rider.in_index(block_of(*g)))
    out_spec = pl.BlockSpec(rider.out_block, lambda *g: rider.out_index(block_of(*g)))
    return in_spec, out_spec, jax.ShapeDtypeStruct(rider.out_shape, BF16)


def _run_rider(nblocks, body, step, src_ref, dst_ref):
    @pl.when(step < nblocks)
    def _():
        body(src_ref, dst_ref)


def _cast_body(src_ref, dst_ref):
    dst_ref[...] = src_ref[...].astype(dst_ref.dtype)


def _pad_halves_body(src_ref, dst_ref):
    ff = src_ref.shape[1] // 2
    ff_pad = dst_ref.shape[1] // 2
    zeros = jnp.zeros((dst_ref.shape[0], ff_pad - ff), dst_ref.dtype)
    for half in range(2):
        dst_ref[:, half * ff_pad:half * ff_pad + ff] = src_ref[:, half * ff:(half + 1) * ff].astype(dst_ref.dtype)
        dst_ref[:, half * ff_pad + ff:(half + 1) * ff_pad] = zeros


def _row_block_rider(src, tr, out_cols, body):
    rows, cols = src.shape
    assert rows % tr == 0
    return _Rider(src, (tr, cols), lambda n: (n, 0), (rows, out_cols), (tr, out_cols), lambda n: (n, 0),
                  rows // tr, body)


def _rmsnorm_kernel(x_ref, g_ref, o_ref):
    o_ref[...] = _group_rmsnorm(x_ref[...], g_ref[...]).astype(o_ref.dtype)


def _rmsnorm(x, g, tm):
    m, d = x.shape
    return pl.pallas_call(
        _rmsnorm_kernel,
        grid=(m // tm,),
        in_specs=[pl.BlockSpec((tm, d), lambda i: (i, 0)), pl.BlockSpec((1, d), lambda i: (0, 0))],
        out_specs=pl.BlockSpec((tm, d), lambda i: (i, 0)),
        out_shape=jax.ShapeDtypeStruct((m, d), BF16),
        compiler_params=_params("parallel"),
        name="rmsnorm",
    )(x, g.reshape(1, d))


def _proj_plain(x, w, col0, ncols, out_dtype, tm, tn, name, rider=None):
    m, k = x.shape
    j0 = col0 // tn
    nj = ncols // tn
    rider_args = None if rider is None else (rider.nblocks, rider.body)

    def kernel(x_ref, w_ref, *refs):
        o_ref = refs[-1] if rider_args is None else refs[1]
        o_ref[...] = jnp.dot(x_ref[...], w_ref[...], preferred_element_type=F32).astype(o_ref.dtype)
        if rider_args is not None:
            _run_rider(*rider_args, pl.program_id(0) * nj + pl.program_id(1), refs[0], refs[2])

    in_specs = [pl.BlockSpec((tm, k), lambda i, j: (i, 0)),
                pl.BlockSpec((k, tn), lambda i, j: (0, j0 + j))]
    out_specs = [pl.BlockSpec((tm, tn), lambda i, j: (i, j))]
    out_shape = [jax.ShapeDtypeStruct((m, ncols), out_dtype)]
    args = [x, w]
    if rider is not None:
        r_in, r_out, r_shape = _rider_io(rider, lambda i, j: i * nj + j, (m // tm) * nj)
        in_specs.append(r_in)
        out_specs.append(r_out)
        out_shape.append(r_shape)
        args.append(rider.src)
    sem = ("parallel", "parallel") if rider is None else ("arbitrary", "arbitrary")
    outs = pl.pallas_call(
        kernel,
        grid=(m // tm, nj),
        in_specs=in_specs,
        out_specs=out_specs,
        out_shape=out_shape,
        compiler_params=_params(*sem),
        name=name,
    )(*args)
    return outs[0] if rider is None else tuple(outs)


def _proj_norm_kernel(x_ref, w_ref, g_ref, o_ref, *, gdim):
    acc = jnp.dot(x_ref[...], w_ref[...], preferred_element_type=F32)
    for c in range(acc.shape[1] // gdim):
        sl = slice(c * gdim, (c + 1) * gdim)
        o_ref[:, sl] = _group_rmsnorm(acc[:, sl], g_ref[...]).astype(o_ref.dtype)


def _proj_norm(x, w, g, col0, ncols, tm, tn, name):
    m, k = x.shape
    gdim = g.shape[0]
    j0 = col0 // tn
    return pl.pallas_call(
        functools.partial(_proj_norm_kernel, gdim=gdim),
        grid=(m // tm, ncols // tn),
        in_specs=[pl.BlockSpec((tm, k), lambda i, j: (i, 0)),
                  pl.BlockSpec((k, tn), lambda i, j: (0, j0 + j)),
                  pl.BlockSpec((1, gdim), lambda i, j: (0, 0))],
        out_specs=pl.BlockSpec((tm, tn), lambda i, j: (i, j)),
        out_shape=jax.ShapeDtypeStruct((m, ncols), BF16),
        compiler_params=_params("parallel", "parallel"),
        name=name,
    )(x, w, g.reshape(1, gdim))


def _proj_qk(x, w, g_cols, cos_t, sa_t, sb_t, seq, rope_dim, tm, tn, rider):
    m, k = x.shape
    ncols = g_cols.shape[0]
    nseq = seq // tm
    nj = ncols // tn
    half = rope_dim // 2
    rider_args = (rider.nblocks, rider.body)

    def kernel(x_ref, w_ref, g_ref, cos_ref, sa_ref, sb_ref, rsrc_ref, o_ref, rdst_ref):
        for r in range(tm // ROW_CHUNK):
            rows = slice(r * ROW_CHUNK, (r + 1) * ROW_CHUNK)
            acc = jnp.dot(x_ref[rows, :], w_ref[...], preferred_element_type=F32)
            for c in range(tn // LANES):
                sl = slice(c * LANES, (c + 1) * LANES)
                y = _group_rmsnorm(acc[:, sl], g_ref[:, sl])
                y = (y * cos_ref[rows, :] + pltpu.roll(y, LANES - half, 1) * sa_ref[rows, :]
                     + pltpu.roll(y, half, 1) * sb_ref[rows, :])
                o_ref[rows, sl] = y.astype(o_ref.dtype)
        _run_rider(*rider_args, pl.program_id(0) * nj + pl.program_id(1), rsrc_ref, rdst_ref)

    tab = pl.BlockSpec((tm, LANES), lambda i, j: (i % nseq, 0))
    r_in, r_out, r_shape = _rider_io(rider, lambda i, j: i * nj + j, (m // tm) * nj)
    return pl.pallas_call(
        kernel,
        grid=(m // tm, nj),
        in_specs=[pl.BlockSpec((tm, k), lambda i, j: (i, 0)),
                  pl.BlockSpec((k, tn), lambda i, j: (0, j)),
                  pl.BlockSpec((1, tn), lambda i, j: (0, j)),
                  tab, tab, tab, r_in],
        out_specs=[pl.BlockSpec((tm, tn), lambda i, j: (i, j)), r_out],
        out_shape=[jax.ShapeDtypeStruct((m, ncols), BF16), r_shape],
        compiler_params=_params("arbitrary", "arbitrary"),
        name="proj_qk",
    )(x, w, g_cols.reshape(1, ncols), cos_t, sa_t, sb_t, rider.src)


def _diff_attn_kernel(q_ref, k_ref, v_ref, lam_ref, gs_ref, rsrc_ref, o_ref, rdst_ref,
                      vt_scr, s_scr, p_scr, a_scr, m_scr, acc_scr,
                      *, tq, tk, dk, lambda_init, rider_args):
    i = pl.program_id(2)
    dv = 2 * dk
    c0 = dk ** -0.5 * LOG2E
    neg = -1e30
    step =(pl.program_id(0) * pl.num_programs(1) + pl.program_id(1)) * pl.num_programs(2) + i
    _run_rider(*rider_args, step, rsrc_ref, rdst_ref)

    @pl.when(i == 0)
    def _():
        for blk in range(vt_scr.shape[0]):
            vt_scr[blk, 0:dv, :] = v_ref[blk * tk:(blk + 1) * tk, :].T
            vt_scr[blk, dv:, :] = jnp.ones((SUM_ROWS, tk), BF16)

    m_scr[...] = jnp.full(m_scr.shape, neg, F32)
    acc_scr[...] = jnp.zeros(acc_scr.shape, F32)

    def scores(blk, slot, qlo=0):
        k = k_ref[pl.ds(pl.multiple_of(blk * tk, tk), tk), :]
        for c in range(2):
            sl = slice(c * dk, (c + 1) * dk)
            s_scr[slot, c, :, qlo:] = lax.dot_general(k[:, sl], q_ref[qlo:, sl], (((1,), (1,)), ((), ())),
                                                      preferred_element_type=F32)

    def softmax(slot, diag=None, qlo=0):
        if diag is not None:
            key_chunk = (diag * tk + lax.broadcasted_iota(jnp.int32, (tk, tq - qlo), 0)) // CHUNK
            q_chunk = (qlo + lax.broadcasted_iota(jnp.int32, (tk, tq - qlo), 1)) // CHUNK
            allowed = key_chunk <= q_chunk
        hk = tk // 4
        for c in range(2):
            s = s_scr[slot, c, :, qlo:]
            if diag is not None:
                s = jnp.where(allowed, s, neg)
            m_prev = m_scr[c, :, qlo:]
            m_new = jnp.maximum(m_prev, jnp.max(s, axis=0, keepdims=True))
            alpha = jnp.exp2((m_prev - m_new) * c0)
            for h in range(4):
                rows = slice(h * hk, (h + 1) * hk)
                sh = s_scr[slot, c, rows, qlo:]
                if diag is not None:
                    sh = jnp.where(allowed[rows], sh, neg)
                p_scr[slot, c, rows, qlo:] = jnp.exp2((sh - m_new) * c0).astype(BF16)
            a_scr[slot, c, :, qlo:] = alpha
            m_scr[c, :, qlo:] = m_new

    def values(slot, blk, qlo=0):
        vt = vt_scr[blk]
        for c in range(2):
            acc_scr[c, :, qlo:] = (a_scr[slot, c, :, qlo:] * acc_scr[c, :, qlo:]
                                   + jnp.dot(vt, p_scr[slot, c, :, qlo:], preferred_element_type=F32))

    scores(2 * i + 1, 0, tk)
    scores(2 * i, 1)
    softmax(0, diag=1, qlo=tk)
    scores(0, 0)
    softmax(1, diag=0)
    values(0, 2 * i + 1, tk)

    def pair(j, last):
        scores(2 * j - 1, 1)
        softmax(0)
        values(1, jnp.where(j == 1, 2 * i, 2 * j - 3))
        if not last:
            scores(2 * j, 0)
        softmax(1)
        values(0, 2 * j - 2)

    def body(j, carry):
        pair(j, False)
        return carry

    lax.fori_loop(1, i, body, 0)

    @pl.when(i >= 1)
    def _():
        pair(i, True)

    values(1, jnp.where(i == 0, 0, 2 * i - 1))

    lam_v = lam_ref[...]
    lam = (jnp.exp(jnp.sum(lam_v[0:1] * lam_v[1:2], axis=-1, keepdims=True))
           - jnp.exp(jnp.sum(lam_v[2:3] * lam_v[3:4], axis=-1, keepdims=True)) + lambda_init)
    o = (acc_scr[0, 0:dv, :] / acc_scr[0, dv:dv + 1, :]
         - lam * (acc_scr[1, 0:dv, :] / acc_scr[1, dv:dv + 1, :]))
    ms = jnp.mean(o * o, axis=0, keepdims=True)
    y = o * lax.rsqrt(ms + EPS) * gs_ref[...] * (1.0 - lambda_init)
    o_ref[...] = y.T.astype(o_ref.dtype)


def _diff_attn(qk, v, lam_vecs, g_subln, batch, seq, heads, dk, lambda_init, tq, rider):
    dv = 2 * dk
    tk = tq // 2
    nq = seq // tq
    kernel = functools.partial(_diff_attn_kernel, tq=tq, tk=tk, dk=dk, lambda_init=lambda_init,
                               rider_args=(rider.nblocks, rider.body))
    r_in, r_out, r_shape = _rider_io(rider, lambda b, h, i: (b * heads + h) * nq + i, batch * heads * nq)
    return pl.pallas_call(
        kernel,
        grid=(batch, heads, nq),
        in_specs=[pl.BlockSpec((tq, dv), lambda b, h, i: (b * nq + i, h)),
                  pl.BlockSpec((seq, dv), lambda b, h, i: (b, heads + h)),
                  pl.BlockSpec((seq, dv), lambda b, h, i: (b, h)),
                  pl.BlockSpec((4, dk), lambda b, h, i: (0, 0)),
                  pl.BlockSpec((dv, 1), lambda b, h, i: (0, 0)), r_in],
        out_specs=[pl.BlockSpec((tq, dv), lambda b, h, i: (b * nq + i, h)), r_out],
        out_shape=[jax.ShapeDtypeStruct((batch * seq, heads * dv), BF16), r_shape],
        scratch_shapes=[pltpu.VMEM((seq // tk, dv + SUM_ROWS, tk), BF16),
                        pltpu.VMEM((2, 2, tk, tq), F32), pltpu.VMEM((2, 2, tk, tq), BF16),
                        pltpu.VMEM((2, 2, 1, tq), F32), pltpu.VMEM((2, 1, tq), F32),
                        pltpu.VMEM((2, dv + SUM_ROWS, tq), F32)],
        compiler_params=_params("arbitrary", "arbitrary", "arbitrary"),
        name="diff_attn",
    )(qk, qk, v, lam_vecs, g_subln.reshape(dv, 1), rider.src)


def _pool_kernel(up_ref, prev_ref, wp_ref, ps_ref, o_ref, ext_scr, *, tm, nseq, halo, gdim):
    i = pl.program_id(0)
    ti = i % nseq
    ext_scr[halo:halo + tm, :] = up_ref[...]
    ext_scr[0:halo, :] = jnp.where(ti == 0, 0.0, prev_ref[...])
    t1 = (ti * tm + 1 + lax.broadcasted_iota(jnp.int32, (tm, 1), 0)).astype(F32)
    for g, win in enumerate(POOL_WINDOWS):
        sl = slice(g * gdim, (g + 1) * gdim)
        cur = ext_scr[halo:halo + tm, sl]
        tot = cur
        for d in range(1, win):
            tot = tot + ext_scr[halo - d:halo - d + tm, sl]
        y = tot / jnp.minimum(t1, float(win)) - cur
        z = jnp.dot(y.astype(BF16), wp_ref[g], preferred_element_type=F32) * ps_ref[g:g + 1, :]
        o_ref[:, sl] = z.astype(o_ref.dtype)


def _pool_mixer(up, w_pool, pool_scale, seq, tm):
    m, width = up.shape
    groups, gdim, _ = w_pool.shape
    halo = 2 * SUBLANES
    assert max(POOL_WINDOWS) <= halo
    nseq = seq // tm
    per = tm // halo
    kernel = functools.partial(_pool_kernel, tm=tm, nseq=nseq, halo=halo, gdim=gdim)
    return pl.pallas_call(
        kernel,
        grid=(m // tm,),
        in_specs=[pl.BlockSpec((tm, width), lambda i: (i, 0)),
                  pl.BlockSpec((halo, width), lambda i: (jnp.maximum(i * per - 1, 0), 0)),
                  pl.BlockSpec((groups, gdim, gdim), lambda i: (0, 0, 0)),
                  pl.BlockSpec((groups, gdim), lambda i: (0, 0))],
        out_specs=pl.BlockSpec((tm, width), lambda i: (i, 0)),
        out_shape=jax.ShapeDtypeStruct((m, width), BF16),
        scratch_shapes=[pltpu.VMEM((tm + halo, width), F32)],
        compiler_params=_params("parallel"),
        name="pool_mixer",
    )(up, up, w_pool, pool_scale)


def _mem_attn_kernel(q_ref, k_ref, v_ref, o_ref, *, heads, hd):
    scale = hd ** -0.5
    for h in range(heads):
        sl = slice(h * hd, (h + 1) * hd)
        s = lax.dot_general(q_ref[:, sl], k_ref[:, sl], (((1,), (1,)), ((), ())),
                            preferred_element_type=F32) * scale
        p = jnp.exp(s - jnp.max(s, axis=-1, keepdims=True))
        l = jnp.sum(p, axis=-1, keepdims=True)
        o = jnp.dot(p.astype(BF16), v_ref[:, sl], preferred_element_type=F32) / l
        o_ref[:, sl] = o.astype(o_ref.dtype)


def _mem_attn(qm, km, vm, batch, seq, mem_len, heads, tq):
    m, width = qm.shape
    nq = seq // tq
    kernel = functools.partial(_mem_attn_kernel, heads=heads, hd=width // heads)
    return pl.pallas_call(
        kernel,
        grid=(batch, nq),
        in_specs=[pl.BlockSpec((tq, width), lambda b, i: (b * nq + i, 0)),
                  pl.BlockSpec((mem_len, width), lambda b, i: (b, 0)),
                  pl.BlockSpec((mem_len, width), lambda b, i: (b, 0))],
        out_specs=pl.BlockSpec((tq, width), lambda b, i: (b * nq + i, 0)),
        out_shape=jax.ShapeDtypeStruct((m, width), BF16),
        compiler_params=_params("parallel", "parallel"),
        name="mem_attn",
    )(qm, km, vm)


def _merge_kernel(ya_ref, yb_ref, yc_ref, wa_ref, wb_ref, wc_ref, ga_ref, gb_ref, gc_ref, o_ref):
    out = jax.nn.sigmoid(ga_ref[...]) * jnp.dot(ya_ref[...], wa_ref[...], preferred_element_type=F32)
    out = out + jax.nn.sigmoid(gb_ref[...]) * jnp.dot(yb_ref[...], wb_ref[...],
                                                      preferred_element_type=F32)
    out = out + jax.nn.sigmoid(gc_ref[...]) * jnp.dot(yc_ref[...], wc_ref[...],
                                                      preferred_element_type=F32)
    o_ref[...] = out.astype(o_ref.dtype)


def _merge(ya, yb, yc, w_a, w_b, w_c, gl, tm, tn):
    m = ya.shape[0]
    d = w_a.shape[1]
    nj = d // tn
    row = lambda a: pl.BlockSpec((tm, a.shape[1]), lambda i, j: (i, 0))
    col = lambda w: pl.BlockSpec((w.shape[0], tn), lambda i, j: (0, j))
    gate = lambda b: pl.BlockSpec((tm, tn), lambda i, j: (i, b * nj + j))
    return pl.pallas_call(
        _merge_kernel,
        grid=(m // tm, nj),
        in_specs=[row(ya), row(yb), row(yc), col(w_a), col(w_b), col(w_c), gate(0), gate(1), gate(2)],
        out_specs=pl.BlockSpec((tm, tn), lambda i, j: (i, j)),
        out_shape=jax.ShapeDtypeStruct((m, d), BF16),
        compiler_params=_params("parallel", "parallel"),
        name="merge",
    )(ya, yb, yc, w_a, w_b, w_c, gl, gl, gl)


def _proj_residual_kernel(a_ref, w_ref, r_ref, o_ref):
    o_ref[...] = r_ref[...] + jnp.dot(a_ref[...], w_ref[...], preferred_element_type=F32)


def _proj_residual(a, w, res, tm, tn):
    m, k = a.shape
    n = w.shape[1]
    return pl.pallas_call(
        _proj_residual_kernel,
        grid=(m // tm, n // tn),
        in_specs=[pl.BlockSpec((tm, k), lambda i, j: (i, 0)),
                  pl.BlockSpec((k, tn), lambda i, j: (0, j)),
                  pl.BlockSpec((tm, tn), lambda i, j: (i, j))],
        out_specs=pl.BlockSpec((tm, tn), lambda i, j: (i, j)),
        out_shape=jax.ShapeDtypeStruct((m, n), F32),
        compiler_params=_params("parallel", "parallel"),
        name="proj_residual",
    )(a, w, res)


def _ffn_up_kernel(h_ref, wg_ref, wv_ref, cwg_ref, cwv_ref, cbg_ref, cbv_ref, o_ref, yg_scr, yv_scr,
                   *, tm, nseq):
    i = pl.program_id(1)
    pad = SUBLANES

    @pl.when(i % nseq == 0)
    def _():
        yg_scr[0:pad, :] = jnp.zeros((pad, yg_scr.shape[1]), F32)
        yv_scr[0:pad, :] = jnp.zeros((pad, yv_scr.shape[1]), F32)

    def conv(r, y_scr, w_ref, cw_ref, cb_ref):
        lo = pad + r * ROW_CHUNK
        y_scr[lo:lo + ROW_CHUNK, :] = jnp.dot(h_ref[r * ROW_CHUNK:(r + 1) * ROW_CHUNK, :], w_ref[...],
                                              preferred_element_type=F32)
        u = cb_ref[...] + y_scr[lo:lo + ROW_CHUNK, :] * cw_ref[CONV_W - 1:CONV_W, :]
        for d in range(1, CONV_W):
            u = u + y_scr[lo - d:lo - d + ROW_CHUNK, :] * cw_ref[CONV_W - 1 - d:CONV_W - d, :]
        return u

    for r in range(tm // ROW_CHUNK):
        gate = conv(r, yg_scr, wg_ref, cwg_ref, cbg_ref)
        val = conv(r, yv_scr, wv_ref, cwv_ref, cbv_ref)
        o_ref[r * ROW_CHUNK:(r + 1) * ROW_CHUNK, :] = (gate * jax.nn.sigmoid(gate) * val).astype(o_ref.dtype)
    yg_scr[0:pad, :] = yg_scr[tm:tm + pad, :]
    yv_scr[0:pad, :] = yv_scr[tm:tm + pad, :]


def _ffn_up(h, w_up, conv_w, conv_b, seq, tm, tn):
    m, k = h.shape
    ff = w_up.shape[1] // 2
    nj = ff // tn
    nseq = seq // tm
    kernel = functools.partial(_ffn_up_kernel, tm=tm, nseq=nseq)
    wspec = lambda half: pl.BlockSpec((k, tn), lambda j, i: (0, half * nj + j))
    cwspec = lambda half: pl.BlockSpec((CONV_W, tn), lambda j, i: (0, half * nj + j))
    cbspec = lambda half: pl.BlockSpec((1, tn), lambda j, i: (0, half * nj + j))
    scratch = pltpu.VMEM((tm + SUBLANES, tn), F32)
    return pl.pallas_call(
        kernel,
        grid=(nj, m // tm),
        in_specs=[pl.BlockSpec((tm, k), lambda j, i: (i, 0)),
                  wspec(0), wspec(1), cwspec(0), cwspec(1), cbspec(0), cbspec(1)],
        out_specs=pl.BlockSpec((tm, tn), lambda j, i: (i, j)),
        out_shape=jax.ShapeDtypeStruct((m, ff), BF16),
        scratch_shapes=[scratch, scratch],
        compiler_params=_params("arbitrary", "arbitrary"),
        name="ffn_up",
    )(h, w_up, w_up, conv_w, conv_w, conv_b.reshape(1, -1), conv_b.reshape(1, -1))


def _ffn_down_kernel(a_ref, w_ref, r_ref, o_ref, acc_scr, *, rem):
    kk = pl.program_id(2)
    last = pl.num_programs(2) - 1

    @pl.when(kk == 0)
    def _():
        acc_scr[...] = r_ref[...]

    @pl.when(kk < last)
    def _():
        acc_scr[...] += jnp.dot(a_ref[...], w_ref[...], preferred_element_type=F32)

    @pl.when(kk == last)
    def _():
        o_ref[...] = acc_scr[...] + jnp.dot(a_ref[:, :rem], w_ref[:rem, :], preferred_element_type=F32)


def _ffn_down(a, w, res, tm, tn, tk):
    m = a.shape[0]
    k, n = w.shape
    nk = pl.cdiv(k, tk)
    rem = k - (nk - 1) * tk
    assert tk % LANES == 0 and rem % LANES == 0 and a.shape[1] >= nk * tk
    return pl.pallas_call(
        functools.partial(_ffn_down_kernel, rem=rem),
        grid=(m // tm, n // tn, nk),
        in_specs=[pl.BlockSpec((tm, tk), lambda i, j, kk: (i, kk)),
                  pl.BlockSpec((tk, tn), lambda i, j, kk: (kk, j)),
                  pl.BlockSpec((tm, tn), lambda i, j, kk: (i, j))],
        out_specs=pl.BlockSpec((tm, tn), lambda i, j, kk: (i, j)),
        out_shape=jax.ShapeDtypeStruct((m, n), F32),
        scratch_shapes=[pltpu.VMEM((tm, tn), F32)],
        compiler_params=_params("parallel", "parallel", "arbitrary"),
        name="ffn_down",
    )(a, w, res)


def _rope_tables(seq, rope_dim):
    half = rope_dim // 2
    pos = jnp.arange(seq, dtype=F32)
    inv = ROPE_THETA ** (-jnp.arange(0, rope_dim, 2, dtype=F32) / rope_dim)
    ang = pos[:, None] * inv[None, :]
    cos, sin = jnp.cos(ang), jnp.sin(ang)
    ones = jnp.ones((seq, LANES - rope_dim), F32)
    zeros = jnp.zeros((seq, LANES - rope_dim), F32)
    zh = jnp.zeros((seq, half), F32)
    cos_t = jnp.concatenate([cos, cos, ones], axis=1)
    sa_t = jnp.concatenate([-sin, zh, zeros], axis=1)
    sb_t = jnp.concatenate([zh, sin, zeros], axis=1)
    return cos_t, sa_t, sb_t


def _pad_halves(a, ff, ff_pad):
    pad = [(0, 0)] * (a.ndim - 1) + [(0, ff_pad - ff)]
    return jnp.concatenate([jnp.pad(a[..., :ff], pad), jnp.pad(a[..., ff:], pad)], axis=-1)


def kernel(x, mem, g_attn_norm, w_in, g_qa, g_ka, lam_q1, lam_k1, lam_q2, lam_k2, g_subln, w_pool,
           pool_scale, g_mem, w_mkv, g_qm, g_km, w_a, w_b, w_c, w_o, g_ffn_norm, w_up, conv_w, conv_b,
           w_down):
    batch, seq, d_model = x.shape
    mem_len = mem.shape[1]
    depth = w_in.shape[0]
    dk = g_qa.shape[1]
    dv = g_subln.shape[1]
    heads = w_a.shape[1] // dv
    qk_width = heads * 2 * dk
    pool_width = w_b.shape[1]
    mem_width = w_c.shape[1]
    mem_hd = g_qm.shape[1]
    rope_dim = dk // 4
    d_ff = w_down.shape[1]
    ff_pad = -(-d_ff // FF_ALIGN) * FF_ALIGN
    assert dk == LANES and dv == 2 * dk
    tm, tn, tnh = ROW_TILE, COL_TILE, HALF_COL_TILE

    m = batch * seq
    xf = x.reshape(m, d_model)
    cos_t, sa_t, sb_t = _rope_tables(seq, rope_dim)
    c_v = 2 * qk_width
    c_up = c_v + heads * dv
    c_qm = c_up + pool_width
    c_gl = c_qm + mem_width

    for l in range(depth):
        lambda_init = 0.8 - 0.6 * math.exp(-0.3 * l)
        g_qk = jnp.concatenate([jnp.tile(g_qa[l], qk_width // dk), jnp.tile(g_ka[l], qk_width // dk)])
        lam_vecs = jnp.stack([lam_q1[l], lam_k1[l], lam_q2[l], lam_k2[l]]).astype(F32)

        w_qk_b = w_in[l][:, :c_v].astype(BF16)
        nrest = (w_in.shape[2] - c_v) // c_v
        rr = W_IN_RIDER_ROWS
        rest_rider = _Rider(w_in[l], (rr, c_v), lambda n: (n // nrest, 1 + n % nrest),
                            (d_model, nrest * c_v), (rr, c_v), lambda n: (n // nrest, n % nrest),
                            (d_model // rr) * nrest, _cast_body)

        h = _rmsnorm(xf, g_attn_norm[l], NORM_ROWS)
        qk, w_rest_b = _proj_qk(h, w_qk_b, g_qk, cos_t, sa_t, sb_t, seq, rope_dim, tm, tn, rest_rider)
        v = _proj_plain(h, w_rest_b, 0, heads * dv, BF16, tm, tn, "proj_v")
        up = _proj_plain(h, w_rest_b, c_up - c_v, pool_width, F32, tm, tn, "proj_up")
        qm = _proj_norm(h, w_rest_b, g_qm[l], c_qm - c_v, mem_width, tm, tn, "proj_qm")
        gl, w_down_b = _proj_plain(h, w_rest_b, c_gl - c_v, 3 * d_model, F32, tm, tn, "proj_gates",
                                   rider=_row_block_rider(w_down[l], W_DOWN_RIDER_ROWS, d_model, _cast_body))

        ya, w_up_b = _diff_attn(qk, v, lam_vecs, g_subln[l], batch, seq, heads, dk, lambda_init, ATTN_TILE,
                                _row_block_rider(w_up[l], W_UP_RIDER_ROWS, 2 * ff_pad, _pad_halves_body))
        yb = _pool_mixer(up, w_pool[l].astype(BF16), pool_scale[l], seq, SIDE_TILE)

        mem_rows = batch * mem_len
        mem_n = _rmsnorm(mem.reshape(mem_rows, d_model), g_mem[l], NORM_ROWS)
        w_mkv_b = w_mkv[l].astype(BF16)
        km = _proj_norm(mem_n, w_mkv_b, g_km[l], 0, mem_width, mem_rows, tn, "proj_km")
        vm = _proj_plain(mem_n, w_mkv_b, mem_width, mem_width, BF16, mem_rows, tn, "proj_vm")
        yc = _mem_attn(qm, km, vm, batch, seq, mem_len, mem_width // mem_hd, SIDE_TILE)

        merged = _merge(ya, yb, yc, w_a[l].astype(BF16), w_b[l].astype(BF16), w_c[l].astype(BF16),
                        gl, tm, tnh)
        xf = _proj_residual(merged, w_o[l].astype(BF16), xf, tm, tnh)

        h2 = _rmsnorm(xf, g_ffn_norm[l], NORM_ROWS)
        act = _ffn_up(h2, w_up_b, _pad_halves(conv_w[l], d_ff, ff_pad),
                      _pad_halves(conv_b[l], d_ff, ff_pad), seq, tm, tnh)
        xf = _ffn_down(act, w_down_b, xf, tm, tn, ff_pad // FFN_DOWN_K_STEPS)

    return xf.reshape(batch, seq, d_model)
```

```python
import functools
import math
from typing import Callable, NamedTuple

import jax
import jax.numpy as jnp
from jax import lax
from jax.experimental import pallas as pl
from jax.experimental.pallas import tpu as pltpu

F32 = jnp.float32
BF16 = jnp.bfloat16

EPS = 1e-6
ROPE_THETA = 500000.0
CHUNK = 64
POOL_WINDOWS = (2, 4, 8, 16)
CONV_W = 3

LANES = 128
SUBLANES = 8
SUM_ROWS = 2 * SUBLANES
VMEM_LIMIT_BYTES = 56 * 1024 * 1024
LOG2E = 1.4426950408889634

ROW_TILE = 1024
COL_TILE = 1024
HALF_COL_TILE = 512
ROW_CHUNK = 256
NORM_ROWS = 512
ATTN_TILE = 1024
SIDE_TILE = 512
FF_ALIGN = 1024
FFN_DOWN_K_STEPS = 4
W_IN_RIDER_ROWS = 256
W_DOWN_RIDER_ROWS = 128
W_UP_RIDER_ROWS = 32


def _params(*sem):
    return pltpu.CompilerParams(dimension_semantics=sem, vmem_limit_bytes=VMEM_LIMIT_BYTES)


def _group_rmsnorm(x, g):
    ms = jnp.mean(x * x, axis=-1, keepdims=True)
    return x * lax.rsqrt(ms + EPS) * g


class _Rider(NamedTuple):
    src: jax.Array
    in_block: tuple
    in_index: Callable
    out_shape: tuple
    out_block: tuple
    out_index: Callable
    nblocks: int
    body: Callable


def _rider_io(rider, step_of, nsteps):
    assert rider.nblocks <= nsteps, "the host call has too few grid steps for this rider"
    block_of = lambda *g: jnp.minimum(step_of(*g), rider.nblocks - 1)
    in_spec = pl.BlockSpec(rider.in_block, lambda *g: rider.in_index(block_of(*g)))
    out_spec = pl.BlockSpec(rider.out_block, lambda *g: rider.out_index(block_of(*g)))
    return in_spec, out_spec, jax.ShapeDtypeStruct(rider.out_shape, BF16)


def _run_rider(nblocks, body, step, src_ref, dst_ref):
    @pl.when(step < nblocks)
    def _():
        body(src_ref, dst_ref)


def _cast_body(src_ref, dst_ref):
    dst_ref[...] = src_ref[...].astype(dst_ref.dtype)


def _pad_halves_body(src_ref, dst_ref):
    ff = src_ref.shape[1] // 2
    ff_pad = dst_ref.shape[1] // 2
    zeros = jnp.zeros((dst_ref.shape[0], ff_pad - ff), dst_ref.dtype)
    for half in range(2):
        dst_ref[:, half * ff_pad:half * ff_pad + ff] = src_ref[:, half * ff:(half + 1) * ff].astype(dst_ref.dtype)
        dst_ref[:, half * ff_pad + ff:(half + 1) * ff_pad] = zeros


def _row_block_rider(src, tr, out_cols, body):
    rows, cols = src.shape
    assert rows % tr == 0
    return _Rider(src, (tr, cols), lambda n: (n, 0), (rows, out_cols), (tr, out_cols), lambda n: (n, 0),
                  rows // tr, body)


def _rmsnorm_kernel(x_ref, g_ref, o_ref):
    o_ref[...] = _group_rmsnorm(x_ref[...], g_ref[...]).astype(o_ref.dtype)


def _rmsnorm(x, g, tm):
    m, d = x.shape
    return pl.pallas_call(
        _rmsnorm_kernel,
        grid=(m // tm,),
        in_specs=[pl.BlockSpec((tm, d), lambda i: (i, 0)), pl.BlockSpec((1, d), lambda i: (0, 0))],
        out_specs=pl.BlockSpec((tm, d), lambda i: (i, 0)),
        out_shape=jax.ShapeDtypeStruct((m, d), BF16),
        compiler_params=_params("parallel"),
        name="rmsnorm",
    )(x, g.reshape(1, d))


def _proj_plain(x, w, col0, ncols, out_dtype, tm, tn, name, rider=None):
    m, k = x.shape
    j0 = col0 // tn
    nj = ncols // tn
    rider_args = None if rider is None else (rider.nblocks, rider.body)

    def kernel(x_ref, w_ref, *refs):
        o_ref = refs[-1] if rider_args is None else refs[1]
        o_ref[...] = jnp.dot(x_ref[...], w_ref[...], preferred_element_type=F32).astype(o_ref.dtype)
        if rider_args is not None:
            _run_rider(*rider_args, pl.program_id(0) * nj + pl.program_id(1), refs[0], refs[2])

    in_specs = [pl.BlockSpec((tm, k), lambda i, j: (i, 0)),
                pl.BlockSpec((k, tn), lambda i, j: (0, j0 + j))]
    out_specs = [pl.BlockSpec((tm, tn), lambda i, j: (i, j))]
    out_shape = [jax.ShapeDtypeStruct((m, ncols), out_dtype)]
    args = [x, w]
    if rider is not None:
        r_in, r_out, r_shape = _rider_io(rider, lambda i, j: i * nj + j, (m // tm) * nj)
        in_specs.append(r_in)
        out_specs.append(r_out)
        out_shape.append(r_shape)
        args.append(rider.src)
    sem = ("parallel", "parallel") if rider is None else ("arbitrary", "arbitrary")
    outs = pl.pallas_call(
        kernel,
        grid=(m // tm, nj),
        in_specs=in_specs,
        out_specs=out_specs,
        out_shape=out_shape,
        compiler_params=_params(*sem),
        name=name,
    )(*args)
    return outs[0] if rider is None else tuple(outs)


def _proj_norm_kernel(x_ref, w_ref, g_ref, o_ref, *, gdim):
    acc = jnp.dot(x_ref[...], w_ref[...], preferred_element_type=F32)
    for c in range(acc.shape[1] // gdim):
        sl = slice(c * gdim, (c + 1) * gdim)
        o_ref[:, sl] = _group_rmsnorm(acc[:, sl], g_ref[...]).astype(o_ref.dtype)


def _proj_norm(x, w, g, col0, ncols, tm, tn, name):
    m, k = x.shape
    gdim = g.shape[0]
    j0 = col0 // tn
    return pl.pallas_call(
        functools.partial(_proj_norm_kernel, gdim=gdim),
        grid=(m // tm, ncols // tn),
        in_specs=[pl.BlockSpec((tm, k), lambda i, j: (i, 0)),
                  pl.BlockSpec((k, tn), lambda i, j: (0, j0 + j)),
                  pl.BlockSpec((1, gdim), lambda i, j: (0, 0))],
        out_specs=pl.BlockSpec((tm, tn), lambda i, j: (i, j)),
        out_shape=jax.ShapeDtypeStruct((m, ncols), BF16),
        compiler_params=_params("parallel", "parallel"),
        name=name,
    )(x, w, g.reshape(1, gdim))


def _proj_qk(x, w, g_cols, cos_t, sa_t, sb_t, seq, rope_dim, tm, tn, rider):
    m, k = x.shape
    ncols = g_cols.shape[0]
    nseq = seq // tm
    nj = ncols // tn
    half = rope_dim // 2
    rider_args = (rider.nblocks, rider.body)

    def kernel(x_ref, w_ref, g_ref, cos_ref, sa_ref, sb_ref, rsrc_ref, o_ref, rdst_ref):
        for r in range(tm // ROW_CHUNK):
            rows = slice(r * ROW_CHUNK, (r + 1) * ROW_CHUNK)
            acc = jnp.dot(x_ref[rows, :], w_ref[...], preferred_element_type=F32)
            for c in range(tn // LANES):
                sl = slice(c * LANES, (c + 1) * LANES)
                y = _group_rmsnorm(acc[:, sl], g_ref[:, sl])
                y = (y * cos_ref[rows, :] + pltpu.roll(y, LANES - half, 1) * sa_ref[rows, :]
                     + pltpu.roll(y, half, 1) * sb_ref[rows, :])
                o_ref[rows, sl] = y.astype(o_ref.dtype)
        _run_rider(*rider_args, pl.program_id(0) * nj + pl.program_id(1), rsrc_ref, rdst_ref)

    tab = pl.BlockSpec((tm, LANES), lambda i, j: (i % nseq, 0))
    r_in, r_out, r_shape = _rider_io(rider, lambda i, j: i * nj + j, (m // tm) * nj)
    return pl.pallas_call(
        kernel,
        grid=(m // tm, nj),
        in_specs=[pl.BlockSpec((tm, k), lambda i, j: (i, 0)),
                  pl.BlockSpec((k, tn), lambda i, j: (0, j)),
                  pl.BlockSpec((1, tn), lambda i, j: (0, j)),
                  tab, tab, tab, r_in],
        out_specs=[pl.BlockSpec((tm, tn), lambda i, j: (i, j)), r_out],
        out_shape=[jax.ShapeDtypeStruct((m, ncols), BF16), r_shape],
        compiler_params=_params("arbitrary", "arbitrary"),
        name="proj_qk",
    )(x, w, g_cols.reshape(1, ncols), cos_t, sa_t, sb_t, rider.src)


def _diff_attn_kernel(q_ref, k_ref, v_ref, lam_ref, gs_ref, rsrc_ref, o_ref, rdst_ref,
                      vt_scr, s_scr, p_scr, a_scr, m_scr, acc_scr,
                      *, tq, tk, dk, lambda_init, rider_args):
    i = pl.program_id(2)
    dv = 2 * dk
    neg = -1e30
    step =(pl.program_id(0) * pl.num_programs(1) + pl.program_id(1)) * pl.num_programs(2) + i
    _run_rider(*rider_args, step, rsrc_ref, rdst_ref)

    @pl.when(i == 0)
    def _():
        for blk in range(vt_scr.shape[0]):
            vt_scr[blk, 0:dv, :] = v_ref[blk * tk:(blk + 1) * tk, :].T
            vt_scr[blk, dv:, :] = jnp.ones((SUM_ROWS, tk), BF16)

    m_scr[...] = jnp.full(m_scr.shape, neg, F32)
    acc_scr[...] = jnp.zeros(acc_scr.shape, F32)

    def scores(blk, slot, qlo=0):
        k = k_ref[pl.ds(pl.multiple_of(blk * tk, tk), tk), :]
        for c in range(2):
            sl = slice(c * dk, (c + 1) * dk)
            s_scr[slot, c, :, qlo:] = lax.dot_general(k[:, sl], q_ref[qlo:, sl], (((1,), (1,)), ((), ())),
                                                      preferred_element_type=F32)

    def softmax(slot, diag=None, qlo=0):
        if diag is not None:
            key_chunk = (diag * tk + lax.broadcasted_iota(jnp.int32, (tk, tq - qlo), 0)) // CHUNK
            q_chunk = (qlo + lax.broadcasted_iota(jnp.int32, (tk, tq - qlo), 1)) // CHUNK
            allowed = key_chunk <= q_chunk
        hk = tk // 4
        for c in range(2):
            s = s_scr[slot, c, :, qlo:]
            if diag is not None:
                s = jnp.where(allowed, s, neg)
            m_prev = m_scr[c, :, qlo:]
            m_new = jnp.maximum(m_prev, jnp.max(s, axis=0, keepdims=True))
            alpha = jnp.exp2(m_prev - m_new)
            for h in range(4):
                rows = slice(h * hk, (h + 1) * hk)
                sh = s_scr[slot, c, rows, qlo:]
                if diag is not None:
                    sh = jnp.where(allowed[rows], sh, neg)
                p_scr[slot, c, rows, qlo:] = jnp.exp2(sh - m_new).astype(BF16)
            a_scr[slot, c, :, qlo:] = alpha
            m_scr[c, :, qlo:] = m_new

    def values(slot, blk, qlo=0):
        vt = vt_scr[blk]
        for c in range(2):
            acc_scr[c, :, qlo:] = (a_scr[slot, c, :, qlo:] * acc_scr[c, :, qlo:]
                                   + jnp.dot(vt, p_scr[slot, c, :, qlo:], preferred_element_type=F32))

    scores(2 * i + 1, 0, tk)
    scores(2 * i, 1)
    softmax(0, diag=1, qlo=tk)
    scores(0, 0)
    softmax(1, diag=0)
    values(0, 2 * i + 1, tk)

    def pair(j, last):
        scores(2 * j - 1, 1)
        softmax(0)
        values(1, jnp.where(j == 1, 2 * i, 2 * j - 3))
        if not last:
            scores(2 * j, 0)
        softmax(1)
        values(0, 2 * j - 2)

    def body(j, carry):
        pair(j, False)
        return carry

    lax.fori_loop(1, i, body, 0)

    @pl.when(i >= 1)
    def _():
        pair(i, True)

    values(1, jnp.where(i == 0, 0, 2 * i - 1))

    lam_v = lam_ref[...]
    lam = (jnp.exp(jnp.sum(lam_v[0:1] * lam_v[1:2], axis=-1, keepdims=True))
           - jnp.exp(jnp.sum(lam_v[2:3] * lam_v[3:4], axis=-1, keepdims=True)) + lambda_init)
    o = (acc_scr[0, 0:dv, :] / acc_scr[0, dv:dv + 1, :]
         - lam * (acc_scr[1, 0:dv, :] / acc_scr[1, dv:dv + 1, :]))
    ms = jnp.mean(o * o, axis=0, keepdims=True)
    y = o * lax.rsqrt(ms + EPS) * gs_ref[...] * (1.0 - lambda_init)
    o_ref[...] = y.T.astype(o_ref.dtype)


def _diff_attn(qk, v, lam_vecs, g_subln, batch, seq, heads, dk, lambda_init, tq, rider):
    dv = 2 * dk
    tk = tq // 2
    nq = seq // tq
    kernel = functools.partial(_diff_attn_kernel, tq=tq, tk=tk, dk=dk, lambda_init=lambda_init,
                               rider_args=(rider.nblocks, rider.body))
    r_in, r_out, r_shape = _rider_io(rider, lambda b, h, i: (b * heads + h) * nq + i, batch * heads * nq)
    return pl.pallas_call(
        kernel,
        grid=(batch, heads, nq),
        in_specs=[pl.BlockSpec((tq, dv), lambda b, h, i: (b * nq + i, h)),
                  pl.BlockSpec((seq, dv), lambda b, h, i: (b, heads + h)),
                  pl.BlockSpec((seq, dv), lambda b, h, i: (b, h)),
                  pl.BlockSpec((4, dk), lambda b, h, i: (0, 0)),
                  pl.BlockSpec((dv, 1), lambda b, h, i: (0, 0)), r_in],
        out_specs=[pl.BlockSpec((tq, dv), lambda b, h, i: (b * nq + i, h)), r_out],
        out_shape=[jax.ShapeDtypeStruct((batch * seq, heads * dv), BF16), r_shape],
        scratch_shapes=[pltpu.VMEM((seq // tk, dv + SUM_ROWS, tk), BF16),
                        pltpu.VMEM((2, 2, tk, tq), F32), pltpu.VMEM((2, 2, tk, tq), BF16),
                        pltpu.VMEM((2, 2, 1, tq), F32), pltpu.VMEM((2, 1, tq), F32),
                        pltpu.VMEM((2, dv + SUM_ROWS, tq), F32)],
        compiler_params=_params("arbitrary", "arbitrary", "arbitrary"),
        name="diff_attn",
    )(qk, qk, v, lam_vecs, g_subln.reshape(dv, 1), rider.src)


def _pool_kernel(up_ref, prev_ref, wp_ref, ps_ref, o_ref, ext_scr, *, tm, nseq, halo, gdim):
    i = pl.program_id(0)
    ti = i % nseq
    ext_scr[halo:halo + tm, :] = up_ref[...]
    ext_scr[0:halo, :] = jnp.where(ti == 0, 0.0, prev_ref[...])
    t1 = (ti * tm + 1 + lax.broadcasted_iota(jnp.int32, (tm, 1), 0)).astype(F32)
    for g, win in enumerate(POOL_WINDOWS):
        sl = slice(g * gdim, (g + 1) * gdim)
        cur = ext_scr[halo:halo + tm, sl]
        tot = cur
        for d in range(1, win):
            tot = tot + ext_scr[halo - d:halo - d + tm, sl]
        y = tot / jnp.minimum(t1, float(win)) - cur
        z = jnp.dot(y.astype(BF16), wp_ref[g], preferred_element_type=F32) * ps_ref[g:g + 1, :]
        o_ref[:, sl] = z.astype(o_ref.dtype)


def _pool_mixer(up, w_pool, pool_scale, seq, tm):
    m, width = up.shape
    groups, gdim, _ = w_pool.shape
    halo = 2 * SUBLANES
    assert max(POOL_WINDOWS) <= halo
    nseq = seq // tm
    per = tm // halo
    kernel = functools.partial(_pool_kernel, tm=tm, nseq=nseq, halo=halo, gdim=gdim)
    return pl.pallas_call(
        kernel,
        grid=(m // tm,),
        in_specs=[pl.BlockSpec((tm, width), lambda i: (i, 0)),
                  pl.BlockSpec((halo, width), lambda i: (jnp.maximum(i * per - 1, 0), 0)),
                  pl.BlockSpec((groups, gdim, gdim), lambda i: (0, 0, 0)),
                  pl.BlockSpec((groups, gdim), lambda i: (0, 0))],
        out_specs=pl.BlockSpec((tm, width), lambda i: (i, 0)),
        out_shape=jax.ShapeDtypeStruct((m, width), BF16),
        scratch_shapes=[pltpu.VMEM((tm + halo, width), F32)],
        compiler_params=_params("parallel"),
        name="pool_mixer",
    )(up, up, w_pool, pool_scale)


def _mem_attn_kernel(q_ref, k_ref, v_ref, o_ref, *, heads, hd):
    scale = hd ** -0.5
    for h in range(heads):
        sl = slice(h * hd, (h + 1) * hd)
        s = lax.dot_general(q_ref[:, sl], k_ref[:, sl], (((1,), (1,)), ((), ())),
                            preferred_element_type=F32) * scale
        p = jnp.exp(s - jnp.max(s, axis=-1, keepdims=True))
        l = jnp.sum(p, axis=-1, keepdims=True)
        o = jnp.dot(p.astype(BF16), v_ref[:, sl], preferred_element_type=F32) / l
        o_ref[:, sl] = o.astype(o_ref.dtype)


def _mem_attn(qm, km, vm, batch, seq, mem_len, heads, tq):
    m, width = qm.shape
    nq = seq // tq
    kernel = functools.partial(_mem_attn_kernel, heads=heads, hd=width // heads)
    return pl.pallas_call(
        kernel,
        grid=(batch, nq),
        in_specs=[pl.BlockSpec((tq, width), lambda b, i: (b * nq + i, 0)),
                  pl.BlockSpec((mem_len, width), lambda b, i: (b, 0)),
                  pl.BlockSpec((mem_len, width), lambda b, i: (b, 0))],
        out_specs=pl.BlockSpec((tq, width), lambda b, i: (b * nq + i, 0)),
        out_shape=jax.ShapeDtypeStruct((m, width), BF16),
        compiler_params=_params("parallel", "parallel"),
        name="mem_attn",
    )(qm, km, vm)


def _merge_kernel(ya_ref, yb_ref, yc_ref, wa_ref, wb_ref, wc_ref, ga_ref, gb_ref, gc_ref, o_ref):
    out = jax.nn.sigmoid(ga_ref[...]) * jnp.dot(ya_ref[...], wa_ref[...], preferred_element_type=F32)
    out = out + jax.nn.sigmoid(gb_ref[...]) * jnp.dot(yb_ref[...], wb_ref[...],
                                                      preferred_element_type=F32)
    out = out + jax.nn.sigmoid(gc_ref[...]) * jnp.dot(yc_ref[...], wc_ref[...],
                                                      preferred_element_type=F32)
    o_ref[...] = out.astype(o_ref.dtype)


def _merge(ya, yb, yc, w_a, w_b, w_c, gl, tm, tn):
    m = ya.shape[0]
    d = w_a.shape[1]
    nj = d // tn
    row = lambda a: pl.BlockSpec((tm, a.shape[1]), lambda i, j: (i, 0))
    col = lambda w: pl.BlockSpec((w.shape[0], tn), lambda i, j: (0, j))
    gate = lambda b: pl.BlockSpec((tm, tn), lambda i, j: (i, b * nj + j))
    return pl.pallas_call(
        _merge_kernel,
        grid=(m // tm, nj),
        in_specs=[row(ya), row(yb), row(yc), col(w_a), col(w_b), col(w_c), gate(0), gate(1), gate(2)],
        out_specs=pl.BlockSpec((tm, tn), lambda i, j: (i, j)),
        out_shape=jax.ShapeDtypeStruct((m, d), BF16),
        compiler_params=_params("parallel", "parallel"),
        name="merge",
    )(ya, yb, yc, w_a, w_b, w_c, gl, gl, gl)


def _proj_residual_kernel(a_ref, w_ref, r_ref, o_ref):
    o_ref[...] = r_ref[...] + jnp.dot(a_ref[...], w_ref[...], preferred_element_type=F32)


def _proj_residual(a, w, res, tm, tn):
    m, k = a.shape
    n = w.shape[1]
    return pl.pallas_call(
        _proj_residual_kernel,
        grid=(m // tm, n // tn),
        in_specs=[pl.BlockSpec((tm, k), lambda i, j: (i, 0)),
                  pl.BlockSpec((k, tn), lambda i, j: (0, j)),
                  pl.BlockSpec((tm, tn), lambda i, j: (i, j))],
        out_specs=pl.BlockSpec((tm, tn), lambda i, j: (i, j)),
        out_shape=jax.ShapeDtypeStruct((m, n), F32),
        compiler_params=_params("parallel", "parallel"),
        name="proj_residual",
    )(a, w, res)


def _ffn_up_kernel(h_ref, wg_ref, wv_ref, cwg_ref, cwv_ref, cbg_ref, cbv_ref, o_ref, yg_scr, yv_scr,
                   *, tm, nseq):
    i = pl.program_id(1)
    pad = SUBLANES

    @pl.when(i % nseq == 0)
    def _():
        yg_scr[0:pad, :] = jnp.zeros((pad, yg_scr.shape[1]), F32)
        yv_scr[0:pad, :] = jnp.zeros((pad, yv_scr.shape[1]), F32)

    def conv(r, y_scr, w_ref, cw_ref, cb_ref):
        lo = pad + r * ROW_CHUNK
        y_scr[lo:lo + ROW_CHUNK, :] = jnp.dot(h_ref[r * ROW_CHUNK:(r + 1) * ROW_CHUNK, :], w_ref[...],
                                              preferred_element_type=F32)
        u = cb_ref[...] + y_scr[lo:lo + ROW_CHUNK, :] * cw_ref[CONV_W - 1:CONV_W, :]
        for d in range(1, CONV_W):
            u = u + y_scr[lo - d:lo - d + ROW_CHUNK, :] * cw_ref[CONV_W - 1 - d:CONV_W - d, :]
        return u

    for r in range(tm // ROW_CHUNK):
        gate = conv(r, yg_scr, wg_ref, cwg_ref, cbg_ref)
        val = conv(r, yv_scr, wv_ref, cwv_ref, cbv_ref)
        o_ref[r * ROW_CHUNK:(r + 1) * ROW_CHUNK, :] = (gate * jax.nn.sigmoid(gate) * val).astype(o_ref.dtype)
    yg_scr[0:pad, :] = yg_scr[tm:tm + pad, :]
    yv_scr[0:pad, :] = yv_scr[tm:tm + pad, :]


def _ffn_up(h, w_up, conv_w, conv_b, seq, tm, tn):
    m, k = h.shape
    ff = w_up.shape[1] // 2
    nj = ff // tn
    nseq = seq // tm
    kernel = functools.partial(_ffn_up_kernel, tm=tm, nseq=nseq)
    wspec = lambda half: pl.BlockSpec((k, tn), lambda j, i: (0, half * nj + j))
    cwspec = lambda half: pl.BlockSpec((CONV_W, tn), lambda j, i: (0, half * nj + j))
    cbspec = lambda half: pl.BlockSpec((1, tn), lambda j, i: (0, half * nj + j))
    scratch = pltpu.VMEM((tm + SUBLANES, tn), F32)
    return pl.pallas_call(
        kernel,
        grid=(nj, m // tm),
        in_specs=[pl.BlockSpec((tm, k), lambda j, i: (i, 0)),
                  wspec(0), wspec(1), cwspec(0), cwspec(1), cbspec(0), cbspec(1)],
        out_specs=pl.BlockSpec((tm, tn), lambda j, i: (i, j)),
        out_shape=jax.ShapeDtypeStruct((m, ff), BF16),
        scratch_shapes=[scratch, scratch],
        compiler_params=_params("arbitrary", "arbitrary"),
        name="ffn_up",
    )(h, w_up, w_up, conv_w, conv_w, conv_b.reshape(1, -1), conv_b.reshape(1, -1))


def _ffn_down_kernel(a_ref, w_ref, r_ref, o_ref, acc_scr, *, rem):
    kk = pl.program_id(2)
    last = pl.num_programs(2) - 1

    @pl.when(kk == 0)
    def _():
        acc_scr[...] = r_ref[...]

    @pl.when(kk < last)
    def _():
        acc_scr[...] += jnp.dot(a_ref[...], w_ref[...], preferred_element_type=F32)

    @pl.when(kk == last)
    def _():
        o_ref[...] = acc_scr[...] + jnp.dot(a_ref[:, :rem], w_ref[:rem, :], preferred_element_type=F32)


def _ffn_down(a, w, res, tm, tn, tk):
    m = a.shape[0]
    k, n = w.shape
    nk = pl.cdiv(k, tk)
    rem = k - (nk - 1) * tk
    assert tk % LANES == 0 and rem % LANES == 0 and a.shape[1] >= nk * tk
    return pl.pallas_call(
        functools.partial(_ffn_down_kernel, rem=rem),
        grid=(m // tm, n // tn, nk),
        in_specs=[pl.BlockSpec((tm, tk), lambda i, j, kk: (i, kk)),
                  pl.BlockSpec((tk, tn), lambda i, j, kk: (kk, j)),
                  pl.BlockSpec((tm, tn), lambda i, j, kk: (i, j))],
        out_specs=pl.BlockSpec((tm, tn), lambda i, j, kk: (i, j)),
        out_shape=jax.ShapeDtypeStruct((m, n), F32),
        scratch_shapes=[pltpu.VMEM((tm, tn), F32)],
        compiler_params=_params("parallel", "parallel", "arbitrary"),
        name="ffn_down",
    )(a, w, res)


def _rope_tables(seq, rope_dim):
    half = rope_dim // 2
    pos = jnp.arange(seq, dtype=F32)
    inv = ROPE_THETA ** (-jnp.arange(0, rope_dim, 2, dtype=F32) / rope_dim)
    ang = pos[:, None] * inv[None, :]
    cos, sin = jnp.cos(ang), jnp.sin(ang)
    ones = jnp.ones((seq, LANES - rope_dim), F32)
    zeros = jnp.zeros((seq, LANES - rope_dim), F32)
    zh = jnp.zeros((seq, half), F32)
    cos_t = jnp.concatenate([cos, cos, ones], axis=1)
    sa_t = jnp.concatenate([-sin, zh, zeros], axis=1)
    sb_t = jnp.concatenate([zh, sin, zeros], axis=1)
    return cos_t, sa_t, sb_t


def _pad_halves(a, ff, ff_pad):
    pad = [(0, 0)] * (a.ndim - 1) + [(0, ff_pad - ff)]
    return jnp.concatenate([jnp.pad(a[..., :ff], pad), jnp.pad(a[..., ff:], pad)], axis=-1)


def kernel(x, mem, g_attn_norm, w_in, g_qa, g_ka, lam_q1, lam_k1, lam_q2, lam_k2, g_subln, w_pool,
           pool_scale, g_mem, w_mkv, g_qm, g_km, w_a, w_b, w_c, w_o, g_ffn_norm, w_up, conv_w, conv_b,
           w_down):
    batch, seq, d_model = x.shape
    mem_len = mem.shape[1]
    depth = w_in.shape[0]
    dk = g_qa.shape[1]
    dv = g_subln.shape[1]
    heads = w_a.shape[1] // dv
    qk_width = heads * 2 * dk
    pool_width = w_b.shape[1]
    mem_width = w_c.shape[1]
    mem_hd = g_qm.shape[1]
    rope_dim = dk // 4
    d_ff = w_down.shape[1]
    ff_pad = -(-d_ff // FF_ALIGN) * FF_ALIGN
    assert dk == LANES and dv == 2 * dk
    tm, tn, tnh = ROW_TILE, COL_TILE, HALF_COL_TILE

    m = batch * seq
    xf = x.reshape(m, d_model)
    cos_t, sa_t, sb_t = _rope_tables(seq, rope_dim)
    c_v = 2 * qk_width
    c_up = c_v + heads * dv
    c_qm = c_up + pool_width
    c_gl = c_qm + mem_width

    for l in range(depth):
        lambda_init = 0.8 - 0.6 * math.exp(-0.3 * l)
        g_qk = jnp.concatenate([jnp.tile(g_qa[l] * (dk ** -0.5 * LOG2E), qk_width // dk),
                                jnp.tile(g_ka[l], qk_width // dk)])
        lam_vecs = jnp.stack([lam_q1[l], lam_k1[l], lam_q2[l], lam_k2[l]]).astype(F32)

        w_qk_b = w_in[l][:, :c_v].astype(BF16)
        nrest = (w_in.shape[2] - c_v) // c_v
        rr = W_IN_RIDER_ROWS
        rest_rider = _Rider(w_in[l], (rr, c_v), lambda n: (n // nrest, 1 + n % nrest),
                            (d_model, nrest * c_v), (rr, c_v), lambda n: (n // nrest, n % nrest),
                            (d_model // rr) * nrest, _cast_body)

        h = _rmsnorm(xf, g_attn_norm[l], NORM_ROWS)
        qk, w_rest_b = _proj_qk(h, w_qk_b, g_qk, cos_t, sa_t, sb_t, seq, rope_dim, tm, tn, rest_rider)
        v = _proj_plain(h, w_rest_b, 0, heads * dv, BF16, tm, tn, "proj_v")
        up = _proj_plain(h, w_rest_b, c_up - c_v, pool_width, F32, tm, tn, "proj_up")
        qm = _proj_norm(h, w_rest_b, g_qm[l], c_qm - c_v, mem_width, tm, tn, "proj_qm")
        gl, w_down_b = _proj_plain(h, w_rest_b, c_gl - c_v, 3 * d_model, F32, tm, tn, "proj_gates",
                                   rider=_row_block_rider(w_down[l], W_DOWN_RIDER_ROWS, d_model, _cast_body))

        ya, w_up_b = _diff_attn(qk, v, lam_vecs, g_subln[l], batch, seq, heads, dk, lambda_init, ATTN_TILE,
                                _row_block_rider(w_up[l], W_UP_RIDER_ROWS, 2 * ff_pad, _pad_halves_body))
        yb = _pool_mixer(up, w_pool[l].astype(BF16), pool_scale[l], seq, SIDE_TILE)

        mem_rows = batch * mem_len
        mem_n = _rmsnorm(mem.reshape(mem_rows, d_model), g_mem[l], NORM_ROWS)
        w_mkv_b = w_mkv[l].astype(BF16)
        km = _proj_norm(mem_n, w_mkv_b, g_km[l], 0, mem_width, mem_rows, tn, "proj_km")
        vm = _proj_plain(mem_n, w_mkv_b, mem_width, mem_width, BF16, mem_rows, tn, "proj_vm")
        yc = _mem_attn(qm, km, vm, batch, seq, mem_len, mem_width // mem_hd, SIDE_TILE)

        merged = _merge(ya, yb, yc, w_a[l].astype(BF16), w_b[l].astype(BF16), w_c[l].astype(BF16),
                        gl, tm, tnh)
        xf = _proj_residual(merged, w_o[l].astype(BF16), xf, tm, tnh)

        h2 = _rmsnorm(xf, g_ffn_norm[l], NORM_ROWS)
        act = _ffn_up(h2, w_up_b, _pad_halves(conv_w[l], d_ff, ff_pad),
                      _pad_halves(conv_b[l], d_ff, ff_pad), seq, tm, tnh)
        xf = _ffn_down(act, w_down_b, xf, tm, tn, ff_pad // FFN_DOWN_K_STEPS)

    return xf.reshape(batch, seq, d_model)
```

```python
import functools
import math
from typing import Callable, NamedTuple

import jax
import jax.numpy as jnp
from jax import lax
from jax.experimental import pallas as pl
from jax.experimental.pallas import tpu as pltpu

F32 = jnp.float32
BF16 = jnp.bfloat16

EPS = 1e-6
ROPE_THETA = 500000.0
CHUNK = 64
POOL_WINDOWS = (2, 4, 8, 16)
CONV_W = 3

LANES = 128
SUBLANES = 8
SUM_ROWS = 2 * SUBLANES
VMEM_LIMIT_BYTES = 56 * 1024 * 1024
LOG2E = 1.4426950408889634

ROW_TILE = 1024
COL_TILE = 1024
HALF_COL_TILE = 512
ROW_CHUNK = 256
NORM_ROWS = 512
ATTN_TILE = 1024
SIDE_TILE = 512
FF_ALIGN = 1024
FFN_DOWN_K_STEPS = 4
W_IN_RIDER_ROWS = 256
W_DOWN_RIDER_ROWS = 128
W_UP_RIDER_ROWS = 32


def _params(*sem):
    return pltpu.CompilerParams(dimension_semantics=sem, vmem_limit_bytes=VMEM_LIMIT_BYTES)


def _group_rmsnorm(x, g):
    ms = jnp.mean(x * x, axis=-1, keepdims=True)
    return x * lax.rsqrt(ms + EPS) * g


class _Rider(NamedTuple):
    src: jax.Array
    in_block: tuple
    in_index: Callable
    out_shape: tuple
    out_block: tuple
    out_index: Callable
    nblocks: int
    body: Callable


def _rider_io(rider, step_of, nsteps):
    assert rider.nblocks <= nsteps, "the host call has too few grid steps for this rider"
    block_of = lambda *g: jnp.minimum(step_of(*g), rider.nblocks - 1)
    in_spec = pl.BlockSpec(rider.in_block, lambda *g: rider.in_index(block_of(*g)))
    out_spec = pl.BlockSpec(rider.out_block, lambda *g: rider.out_index(block_of(*g)))
    return in_spec, out_spec, jax.ShapeDtypeStruct(rider.out_shape, BF16)


def _run_rider(nblocks, body, step, src_ref, dst_ref):
    @pl.when(step < nblocks)
    def _():
        body(src_ref, dst_ref)


def _cast_body(src_ref, dst_ref):
    dst_ref[...] = src_ref[...].astype(dst_ref.dtype)


def _pad_halves_body(src_ref, dst_ref):
    ff = src_ref.shape[1] // 2
    ff_pad = dst_ref.shape[1] // 2
    zeros = jnp.zeros((dst_ref.shape[0], ff_pad - ff), dst_ref.dtype)
    for half in range(2):
        dst_ref[:, half * ff_pad:half * ff_pad + ff] = src_ref[:, half * ff:(half + 1) * ff].astype(dst_ref.dtype)
        dst_ref[:, half * ff_pad + ff:(half + 1) * ff_pad] = zeros


def _row_block_rider(src, tr, out_cols, body):
    rows, cols = src.shape
    assert rows % tr == 0
    return _Rider(src, (tr, cols), lambda n: (n, 0), (rows, out_cols), (tr, out_cols), lambda n: (n, 0),
                  rows // tr, body)


def _rmsnorm_kernel(x_ref, g_ref, o_ref):
    o_ref[...] = _group_rmsnorm(x_ref[...], g_ref[...]).astype(o_ref.dtype)


def _rmsnorm(x, g, tm):
    m, d = x.shape
    return pl.pallas_call(
        _rmsnorm_kernel,
        grid=(m // tm,),
        in_specs=[pl.BlockSpec((tm, d), lambda i: (i, 0)), pl.BlockSpec((1, d), lambda i: (0, 0))],
        out_specs=pl.BlockSpec((tm, d), lambda i: (i, 0)),
        out_shape=jax.ShapeDtypeStruct((m, d), BF16),
        compiler_params=_params("parallel"),
        name="rmsnorm",
    )(x, g.reshape(1, d))


def _proj_plain(x, w, col0, ncols, out_dtype, tm, tn, name, rider=None):
    m, k = x.shape
    j0 = col0 // tn
    nj = ncols // tn
    rider_args = None if rider is None else (rider.nblocks, rider.body)

    def kernel(x_ref, w_ref, *refs):
        o_ref = refs[-1] if rider_args is None else refs[1]
        o_ref[...] = jnp.dot(x_ref[...], w_ref[...], preferred_element_type=F32).astype(o_ref.dtype)
        if rider_args is not None:
            _run_rider(*rider_args, pl.program_id(0) * nj + pl.program_id(1), refs[0], refs[2])

    in_specs = [pl.BlockSpec((tm, k), lambda i, j: (i, 0)),
                pl.BlockSpec((k, tn), lambda i, j: (0, j0 + j))]
    out_specs = [pl.BlockSpec((tm, tn), lambda i, j: (i, j))]
    out_shape = [jax.ShapeDtypeStruct((m, ncols), out_dtype)]
    args = [x, w]
    if rider is not None:
        r_in, r_out, r_shape = _rider_io(rider, lambda i, j: i * nj + j, (m // tm) * nj)
        in_specs.append(r_in)
        out_specs.append(r_out)
        out_shape.append(r_shape)
        args.append(rider.src)
    sem = ("parallel", "parallel") if rider is None else ("arbitrary", "arbitrary")
    outs = pl.pallas_call(
        kernel,
        grid=(m // tm, nj),
        in_specs=in_specs,
        out_specs=out_specs,
        out_shape=out_shape,
        compiler_params=_params(*sem),
        name=name,
    )(*args)
    return outs[0] if rider is None else tuple(outs)


def _proj_norm_kernel(x_ref, w_ref, g_ref, o_ref, *, gdim):
    acc = jnp.dot(x_ref[...], w_ref[...], preferred_element_type=F32)
    for c in range(acc.shape[1] // gdim):
        sl = slice(c * gdim, (c + 1) * gdim)
        o_ref[:, sl] = _group_rmsnorm(acc[:, sl], g_ref[...]).astype(o_ref.dtype)


def _proj_norm(x, w, g, col0, ncols, tm, tn, name):
    m, k = x.shape
    gdim = g.shape[0]
    j0 = col0 // tn
    return pl.pallas_call(
        functools.partial(_proj_norm_kernel, gdim=gdim),
        grid=(m // tm, ncols // tn),
        in_specs=[pl.BlockSpec((tm, k), lambda i, j: (i, 0)),
                  pl.BlockSpec((k, tn), lambda i, j: (0, j0 + j)),
                  pl.BlockSpec((1, gdim), lambda i, j: (0, 0))],
        out_specs=pl.BlockSpec((tm, tn), lambda i, j: (i, j)),
        out_shape=jax.ShapeDtypeStruct((m, ncols), BF16),
        compiler_params=_params("parallel", "parallel"),
        name=name,
    )(x, w, g.reshape(1, gdim))


def _proj_qk(x, w, g_cols, cos_t, sa_t, sb_t, seq, rope_dim, tm, tn, rider):
    m, k = x.shape
    ncols = g_cols.shape[0]
    nseq = seq // tm
    nj = ncols // tn
    half = rope_dim // 2
    rider_args = (rider.nblocks, rider.body)

    def kernel(x_ref, w_ref, g_ref, cos_ref, sa_ref, sb_ref, rsrc_ref, o_ref, rdst_ref):
        for r in range(tm // ROW_CHUNK):
            rows = slice(r * ROW_CHUNK, (r + 1) * ROW_CHUNK)
            acc = jnp.dot(x_ref[rows, :], w_ref[...], preferred_element_type=F32)
            for c in range(tn // LANES):
                sl = slice(c * LANES, (c + 1) * LANES)
                y = _group_rmsnorm(acc[:, sl], g_ref[:, sl])
                y = (y * cos_ref[rows, :] + pltpu.roll(y, LANES - half, 1) * sa_ref[rows, :]
                     + pltpu.roll(y, half, 1) * sb_ref[rows, :])
                o_ref[rows, sl] = y.astype(o_ref.dtype)
        _run_rider(*rider_args, pl.program_id(0) * nj + pl.program_id(1), rsrc_ref, rdst_ref)

    tab = pl.BlockSpec((tm, LANES), lambda i, j: (i % nseq, 0))
    r_in, r_out, r_shape = _rider_io(rider, lambda i, j: i * nj + j, (m // tm) * nj)
    return pl.pallas_call(
        kernel,
        grid=(m // tm, nj),
        in_specs=[pl.BlockSpec((tm, k), lambda i, j: (i, 0)),
                  pl.BlockSpec((k, tn), lambda i, j: (0, j)),
                  pl.BlockSpec((1, tn), lambda i, j: (0, j)),
                  tab, tab, tab, r_in],
        out_specs=[pl.BlockSpec((tm, tn), lambda i, j: (i, j)), r_out],
        out_shape=[jax.ShapeDtypeStruct((m, ncols), BF16), r_shape],
        compiler_params=_params("arbitrary", "arbitrary"),
        name="proj_qk",
    )(x, w, g_cols.reshape(1, ncols), cos_t, sa_t, sb_t, rider.src)


def _diff_attn_kernel(q_ref, k_ref, v_ref, lam_ref, gs_ref, rsrc_ref, o_ref, rdst_ref,
                      vt_scr, s_scr, p_scr, a_scr, mx_scr, m_scr, acc_scr,
                      *, tq, tk, dk, lambda_init, rider_args):
    i = pl.program_id(2)
    dv = 2 * dk
    neg = -1e30
    step =(pl.program_id(0) * pl.num_programs(1) + pl.program_id(1)) * pl.num_programs(2) + i
    _run_rider(*rider_args, step, rsrc_ref, rdst_ref)

    @pl.when(i == 0)
    def _():
        for blk in range(vt_scr.shape[0]):
            vt_scr[blk, 0:dv, :] = v_ref[blk * tk:(blk + 1) * tk, :].T
            vt_scr[blk, dv:, :] = jnp.ones((SUM_ROWS, tk), BF16)

    m_scr[...] = jnp.full(m_scr.shape, neg, F32)
    acc_scr[...] = jnp.zeros(acc_scr.shape, F32)

    def scores(blk, slot, diag=None, qlo=0):
        k = k_ref[pl.ds(pl.multiple_of(blk * tk, tk), tk), :]
        if diag is not None:
            key_chunk = (diag * tk + lax.broadcasted_iota(jnp.int32, (tk, tq - qlo), 0)) // CHUNK
            q_chunk = (qlo + lax.broadcasted_iota(jnp.int32, (tk, tq - qlo), 1)) // CHUNK
            allowed = key_chunk <= q_chunk
        for c in range(2):
            sl = slice(c * dk, (c + 1) * dk)
            s = lax.dot_general(k[:, sl], q_ref[qlo:, sl], (((1,), (1,)), ((), ())),
                                preferred_element_type=F32)
            if diag is not None:
                s = jnp.where(allowed, s, neg)
            s_scr[slot, c, :, qlo:] = s
            mx_scr[slot, c, :, qlo:] = jnp.max(s, axis=0, keepdims=True)

    def softmax(slot, qlo=0):
        hk = tk // 4
        for c in range(2):
            m_prev = m_scr[c, :, qlo:]
            m_new = jnp.maximum(m_prev, mx_scr[slot, c, :, qlo:])
            for h in range(4):
                rows = slice(h * hk, (h + 1) * hk)
                p_scr[slot, c, rows, qlo:] = jnp.exp2(s_scr[slot, c, rows, qlo:] - m_new).astype(BF16)
            a_scr[slot, c, :, qlo:] = jnp.exp2(m_prev - m_new)
            m_scr[c, :, qlo:] = m_new

    def values(slot, blk, qlo=0):
        vt = vt_scr[blk]
        for c in range(2):
            acc_scr[c, :, qlo:] = (a_scr[slot, c, :, qlo:] * acc_scr[c, :, qlo:]
                                   + jnp.dot(vt, p_scr[slot, c, :, qlo:], preferred_element_type=F32))

    scores(2 * i + 1, 0, diag=1, qlo=tk)
    scores(2 * i, 1, diag=0)
    softmax(0, qlo=tk)
    scores(0, 0)
    softmax(1)
    values(0, 2 * i + 1, tk)

    def pair(j, last):
        scores(2 * j - 1, 1)
        softmax(0)
        values(1, jnp.where(j == 1, 2 * i, 2 * j - 3))
        if not last:
            scores(2 * j, 0)
        softmax(1)
        values(0, 2 * j - 2)

    def body(j, carry):
        pair(j, False)
        return carry

    lax.fori_loop(1, i, body, 0)

    @pl.when(i >= 1)
    def _():
        pair(i, True)

    values(1, jnp.where(i == 0, 0, 2 * i - 1))

    lam_v = lam_ref[...]
    lam = (jnp.exp(jnp.sum(lam_v[0:1] * lam_v[1:2], axis=-1, keepdims=True))
           - jnp.exp(jnp.sum(lam_v[2:3] * lam_v[3:4], axis=-1, keepdims=True)) + lambda_init)
    o = (acc_scr[0, 0:dv, :] / acc_scr[0, dv:dv + 1, :]
         - lam * (acc_scr[1, 0:dv, :] / acc_scr[1, dv:dv + 1, :]))
    ms = jnp.mean(o * o, axis=0, keepdims=True)
    y = o * lax.rsqrt(ms + EPS) * gs_ref[...] * (1.0 - lambda_init)
    o_ref[...] = y.T.astype(o_ref.dtype)


def _diff_attn(qk, v, lam_vecs, g_subln, batch, seq, heads, dk, lambda_init, tq, rider):
    dv = 2 * dk
    tk = tq // 2
    nq = seq // tq
    kernel = functools.partial(_diff_attn_kernel, tq=tq, tk=tk, dk=dk, lambda_init=lambda_init,
                               rider_args=(rider.nblocks, rider.body))
    r_in, r_out, r_shape = _rider_io(rider, lambda b, h, i: (b * heads + h) * nq + i, batch * heads * nq)
    return pl.pallas_call(
        kernel,
        grid=(batch, heads, nq),
        in_specs=[pl.BlockSpec((tq, dv), lambda b, h, i: (b * nq + i, h)),
                  pl.BlockSpec((seq, dv), lambda b, h, i: (b, heads + h)),
                  pl.BlockSpec((seq, dv), lambda b, h, i: (b, h)),
                  pl.BlockSpec((4, dk), lambda b, h, i: (0, 0)),
                  pl.BlockSpec((dv, 1), lambda b, h, i: (0, 0)), r_in],
        out_specs=[pl.BlockSpec((tq, dv), lambda b, h, i: (b * nq + i, h)), r_out],
        out_shape=[jax.ShapeDtypeStruct((batch * seq, heads * dv), BF16), r_shape],
        scratch_shapes=[pltpu.VMEM((seq // tk, dv + SUM_ROWS, tk), BF16),
                        pltpu.VMEM((2, 2, tk, tq), F32), pltpu.VMEM((2, 2, tk, tq), BF16),
                        pltpu.VMEM((2, 2, 1, tq), F32), pltpu.VMEM((2, 2, 1, tq), F32),
                        pltpu.VMEM((2, 1, tq), F32), pltpu.VMEM((2, dv + SUM_ROWS, tq), F32)],
        compiler_params=_params("arbitrary", "arbitrary", "arbitrary"),
        name="diff_attn",
    )(qk, qk, v, lam_vecs, g_subln.reshape(dv, 1), rider.src)


def _pool_kernel(up_ref, prev_ref, wp_ref, ps_ref, o_ref, ext_scr, *, tm, nseq, halo, gdim):
    i = pl.program_id(0)
    ti = i % nseq
    ext_scr[halo:halo + tm, :] = up_ref[...]
    ext_scr[0:halo, :] = jnp.where(ti == 0, 0.0, prev_ref[...])
    t1 = (ti * tm + 1 + lax.broadcasted_iota(jnp.int32, (tm, 1), 0)).astype(F32)
    for g, win in enumerate(POOL_WINDOWS):
        sl = slice(g * gdim, (g + 1) * gdim)
        cur = ext_scr[halo:halo + tm, sl]
        tot = cur
        for d in range(1, win):
            tot = tot + ext_scr[halo - d:halo - d + tm, sl]
        y = tot / jnp.minimum(t1, float(win)) - cur
        z = jnp.dot(y.astype(BF16), wp_ref[g], preferred_element_type=F32) * ps_ref[g:g + 1, :]
        o_ref[:, sl] = z.astype(o_ref.dtype)


def _pool_mixer(up, w_pool, pool_scale, seq, tm):
    m, width = up.shape
    groups, gdim, _ = w_pool.shape
    halo = 2 * SUBLANES
    assert max(POOL_WINDOWS) <= halo
    nseq = seq // tm
    per = tm // halo
    kernel = functools.partial(_pool_kernel, tm=tm, nseq=nseq, halo=halo, gdim=gdim)
    return pl.pallas_call(
        kernel,
        grid=(m // tm,),
        in_specs=[pl.BlockSpec((tm, width), lambda i: (i, 0)),
                  pl.BlockSpec((halo, width), lambda i: (jnp.maximum(i * per - 1, 0), 0)),
                  pl.BlockSpec((groups, gdim, gdim), lambda i: (0, 0, 0)),
                  pl.BlockSpec((groups, gdim), lambda i: (0, 0))],
        out_specs=pl.BlockSpec((tm, width), lambda i: (i, 0)),
        out_shape=jax.ShapeDtypeStruct((m, width), BF16),
        scratch_shapes=[pltpu.VMEM((tm + halo, width), F32)],
        compiler_params=_params("parallel"),
        name="pool_mixer",
    )(up, up, w_pool, pool_scale)


def _mem_attn_kernel(q_ref, k_ref, v_ref, o_ref, *, heads, hd):
    scale = hd ** -0.5
    for h in range(heads):
        sl = slice(h * hd, (h + 1) * hd)
        s = lax.dot_general(q_ref[:, sl], k_ref[:, sl], (((1,), (1,)), ((), ())),
                            preferred_element_type=F32) * scale
        p = jnp.exp(s - jnp.max(s, axis=-1, keepdims=True))
        l = jnp.sum(p, axis=-1, keepdims=True)
        o = jnp.dot(p.astype(BF16), v_ref[:, sl], preferred_element_type=F32) / l
        o_ref[:, sl] = o.astype(o_ref.dtype)


def _mem_attn(qm, km, vm, batch, seq, mem_len, heads, tq):
    m, width = qm.shape
    nq = seq // tq
    kernel = functools.partial(_mem_attn_kernel, heads=heads, hd=width // heads)
    return pl.pallas_call(
        kernel,
        grid=(batch, nq),
        in_specs=[pl.BlockSpec((tq, width), lambda b, i: (b * nq + i, 0)),
                  pl.BlockSpec((mem_len, width), lambda b, i: (b, 0)),
                  pl.BlockSpec((mem_len, width), lambda b, i: (b, 0))],
        out_specs=pl.BlockSpec((tq, width), lambda b, i: (b * nq + i, 0)),
        out_shape=jax.ShapeDtypeStruct((m, width), BF16),
        compiler_params=_params("parallel", "parallel"),
        name="mem_attn",
    )(qm, km, vm)


def _merge_kernel(ya_ref, yb_ref, yc_ref, wa_ref, wb_ref, wc_ref, ga_ref, gb_ref, gc_ref, o_ref):
    out = jax.nn.sigmoid(ga_ref[...]) * jnp.dot(ya_ref[...], wa_ref[...], preferred_element_type=F32)
    out = out + jax.nn.sigmoid(gb_ref[...]) * jnp.dot(yb_ref[...], wb_ref[...],
                                                      preferred_element_type=F32)
    out = out + jax.nn.sigmoid(gc_ref[...]) * jnp.dot(yc_ref[...], wc_ref[...],
                                                      preferred_element_type=F32)
    o_ref[...] = out.astype(o_ref.dtype)


def _merge(ya, yb, yc, w_a, w_b, w_c, gl, tm, tn):
    m = ya.shape[0]
    d = w_a.shape[1]
    nj = d // tn
    row = lambda a: pl.BlockSpec((tm, a.shape[1]), lambda i, j: (i, 0))
    col = lambda w: pl.BlockSpec((w.shape[0], tn), lambda i, j: (0, j))
    gate = lambda b: pl.BlockSpec((tm, tn), lambda i, j: (i, b * nj + j))
    return pl.pallas_call(
        _merge_kernel,
        grid=(m // tm, nj),
        in_specs=[row(ya), row(yb), row(yc), col(w_a), col(w_b), col(w_c), gate(0), gate(1), gate(2)],
        out_specs=pl.BlockSpec((tm, tn), lambda i, j: (i, j)),
        out_shape=jax.ShapeDtypeStruct((m, d), BF16),
        compiler_params=_params("parallel", "parallel"),
        name="merge",
    )(ya, yb, yc, w_a, w_b, w_c, gl, gl, gl)


def _proj_residual_kernel(a_ref, w_ref, r_ref, o_ref):
    o_ref[...] = r_ref[...] + jnp.dot(a_ref[...], w_ref[...], preferred_element_type=F32)


def _proj_residual(a, w, res, tm, tn):
    m, k = a.shape
    n = w.shape[1]
    return pl.pallas_call(
        _proj_residual_kernel,
        grid=(m // tm, n // tn),
        in_specs=[pl.BlockSpec((tm, k), lambda i, j: (i, 0)),
                  pl.BlockSpec((k, tn), lambda i, j: (0, j)),
                  pl.BlockSpec((tm, tn), lambda i, j: (i, j))],
        out_specs=pl.BlockSpec((tm, tn), lambda i, j: (i, j)),
        out_shape=jax.ShapeDtypeStruct((m, n), F32),
        compiler_params=_params("parallel", "parallel"),
        name="proj_residual",
    )(a, w, res)


def _ffn_up_kernel(h_ref, wg_ref, wv_ref, cwg_ref, cwv_ref, cbg_ref, cbv_ref, o_ref, yg_scr, yv_scr,
                   *, tm, nseq):
    i = pl.program_id(1)
    pad = SUBLANES

    @pl.when(i % nseq == 0)
    def _():
        yg_scr[0:pad, :] = jnp.zeros((pad, yg_scr.shape[1]), F32)
        yv_scr[0:pad, :] = jnp.zeros((pad, yv_scr.shape[1]), F32)

    def conv(r, y_scr, w_ref, cw_ref, cb_ref):
        lo = pad + r * ROW_CHUNK
        y_scr[lo:lo + ROW_CHUNK, :] = jnp.dot(h_ref[r * ROW_CHUNK:(r + 1) * ROW_CHUNK, :], w_ref[...],
                                              preferred_element_type=F32)
        u = cb_ref[...] + y_scr[lo:lo + ROW_CHUNK, :] * cw_ref[CONV_W - 1:CONV_W, :]
        for d in range(1, CONV_W):
            u = u + y_scr[lo - d:lo - d + ROW_CHUNK, :] * cw_ref[CONV_W - 1 - d:CONV_W - d, :]
        return u

    for r in range(tm // ROW_CHUNK):
        gate = conv(r, yg_scr, wg_ref, cwg_ref, cbg_ref)
        val = conv(r, yv_scr, wv_ref, cwv_ref, cbv_ref)
        o_ref[r * ROW_CHUNK:(r + 1) * ROW_CHUNK, :] = (gate * jax.nn.sigmoid(gate) * val).astype(o_ref.dtype)
    yg_scr[0:pad, :] = yg_scr[tm:tm + pad, :]
    yv_scr[0:pad, :] = yv_scr[tm:tm + pad, :]


def _ffn_up(h, w_up, conv_w, conv_b, seq, tm, tn):
    m, k = h.shape
    ff = w_up.shape[1] // 2
    nj = ff // tn
    nseq = seq // tm
    kernel = functools.partial(_ffn_up_kernel, tm=tm, nseq=nseq)
    wspec = lambda half: pl.BlockSpec((k, tn), lambda j, i: (0, half * nj + j))
    cwspec = lambda half: pl.BlockSpec((CONV_W, tn), lambda j, i: (0, half * nj + j))
    cbspec = lambda half: pl.BlockSpec((1, tn), lambda j, i: (0, half * nj + j))
    scratch = pltpu.VMEM((tm + SUBLANES, tn), F32)
    return pl.pallas_call(
        kernel,
        grid=(nj, m // tm),
        in_specs=[pl.BlockSpec((tm, k), lambda j, i: (i, 0)),
                  wspec(0), wspec(1), cwspec(0), cwspec(1), cbspec(0), cbspec(1)],
        out_specs=pl.BlockSpec((tm, tn), lambda j, i: (i, j)),
        out_shape=jax.ShapeDtypeStruct((m, ff), BF16),
        scratch_shapes=[scratch, scratch],
        compiler_params=_params("arbitrary", "arbitrary"),
        name="ffn_up",
    )(h, w_up, w_up, conv_w, conv_w, conv_b.reshape(1, -1), conv_b.reshape(1, -1))


def _ffn_down_kernel(a_ref, w_ref, r_ref, o_ref, acc_scr, *, rem):
    kk = pl.program_id(2)
    last = pl.num_programs(2) - 1

    @pl.when(kk == 0)
    def _():
        acc_scr[...] = r_ref[...]

    @pl.when(kk < last)
    def _():
        acc_scr[...] += jnp.dot(a_ref[...], w_ref[...], preferred_element_type=F32)

    @pl.when(kk == last)
    def _():
        o_ref[...] = acc_scr[...] + jnp.dot(a_ref[:, :rem], w_ref[:rem, :], preferred_element_type=F32)


def _ffn_down(a, w, res, tm, tn, tk):
    m = a.shape[0]
    k, n = w.shape
    nk = pl.cdiv(k, tk)
    rem = k - (nk - 1) * tk
    assert tk % LANES == 0 and rem % LANES == 0 and a.shape[1] >= nk * tk
    return pl.pallas_call(
        functools.partial(_ffn_down_kernel, rem=rem),
        grid=(m // tm, n // tn, nk),
        in_specs=[pl.BlockSpec((tm, tk), lambda i, j, kk: (i, kk)),
                  pl.BlockSpec((tk, tn), lambda i, j, kk: (kk, j)),
                  pl.BlockSpec((tm, tn), lambda i, j, kk: (i, j))],
        out_specs=pl.BlockSpec((tm, tn), lambda i, j, kk: (i, j)),
        out_shape=jax.ShapeDtypeStruct((m, n), F32),
        scratch_shapes=[pltpu.VMEM((tm, tn), F32)],
        compiler_params=_params("parallel", "parallel", "arbitrary"),
        name="ffn_down",
    )(a, w, res)


def _rope_tables(seq, rope_dim):
    half = rope_dim // 2
    pos = jnp.arange(seq, dtype=F32)
    inv = ROPE_THETA ** (-jnp.arange(0, rope_dim, 2, dtype=F32) / rope_dim)
    ang = pos[:, None] * inv[None, :]
    cos, sin = jnp.cos(ang), jnp.sin(ang)
    ones = jnp.ones((seq, LANES - rope_dim), F32)
    zeros = jnp.zeros((seq, LANES - rope_dim), F32)
    zh = jnp.zeros((seq, half), F32)
    cos_t = jnp.concatenate([cos, cos, ones], axis=1)
    sa_t = jnp.concatenate([-sin, zh, zeros], axis=1)
    sb_t = jnp.concatenate([zh, sin, zeros], axis=1)
    return cos_t, sa_t, sb_t


def _pad_halves(a, ff, ff_pad):
    pad = [(0, 0)] * (a.ndim - 1) + [(0, ff_pad - ff)]
    return jnp.concatenate([jnp.pad(a[..., :ff], pad), jnp.pad(a[..., ff:], pad)], axis=-1)


def kernel(x, mem, g_attn_norm, w_in, g_qa, g_ka, lam_q1, lam_k1, lam_q2, lam_k2, g_subln, w_pool,
           pool_scale, g_mem, w_mkv, g_qm, g_km, w_a, w_b, w_c, w_o, g_ffn_norm, w_up, conv_w, conv_b,
           w_down):
    batch, seq, d_model = x.shape
    mem_len = mem.shape[1]
    depth = w_in.shape[0]
    dk = g_qa.shape[1]
    dv = g_subln.shape[1]
    heads = w_a.shape[1] // dv
    qk_width = heads * 2 * dk
    pool_width = w_b.shape[1]
    mem_width = w_c.shape[1]
    mem_hd = g_qm.shape[1]
    rope_dim = dk // 4
    d_ff = w_down.shape[1]
    ff_pad = -(-d_ff // FF_ALIGN) * FF_ALIGN
    assert dk == LANES and dv == 2 * dk
    tm, tn, tnh = ROW_TILE, COL_TILE, HALF_COL_TILE

    m = batch * seq
    xf = x.reshape(m, d_model)
    cos_t, sa_t, sb_t = _rope_tables(seq, rope_dim)
    c_v = 2 * qk_width
    c_up = c_v + heads * dv
    c_qm = c_up + pool_width
    c_gl = c_qm + mem_width

    for l in range(depth):
        lambda_init = 0.8 - 0.6 * math.exp(-0.3 * l)
        g_qk = jnp.concatenate([jnp.tile(g_qa[l] * (dk ** -0.5 * LOG2E), qk_width // dk),
                                jnp.tile(g_ka[l], qk_width // dk)])
        lam_vecs = jnp.stack([lam_q1[l], lam_k1[l], lam_q2[l], lam_k2[l]]).astype(F32)

        w_qk_b = w_in[l][:, :c_v].astype(BF16)
        nrest = (w_in.shape[2] - c_v) // c_v
        rr = W_IN_RIDER_ROWS
        rest_rider = _Rider(w_in[l], (rr, c_v), lambda n: (n // nrest, 1 + n % nrest),
                            (d_model, nrest * c_v), (rr, c_v), lambda n: (n // nrest, n % nrest),
                            (d_model // rr) * nrest, _cast_body)

        h = _rmsnorm(xf, g_attn_norm[l], NORM_ROWS)
        qk, w_rest_b = _proj_qk(h, w_qk_b, g_qk, cos_t, sa_t, sb_t, seq, rope_dim, tm, tn, rest_rider)
        v = _proj_plain(h, w_rest_b, 0, heads * dv, BF16, tm, tn, "proj_v")
        up = _proj_plain(h, w_rest_b, c_up - c_v, pool_width, F32, tm, tn, "proj_up")
        qm = _proj_norm(h, w_rest_b, g_qm[l], c_qm - c_v, mem_width, tm, tn, "proj_qm")
        gl, w_down_b = _proj_plain(h, w_rest_b, c_gl - c_v, 3 * d_model, F32, tm, tn, "proj_gates",
                                   rider=_row_block_rider(w_down[l], W_DOWN_RIDER_ROWS, d_model, _cast_body))

        ya, w_up_b = _diff_attn(qk, v, lam_vecs, g_subln[l], batch, seq, heads, dk, lambda_init, ATTN_TILE,
                                _row_block_rider(w_up[l], W_UP_RIDER_ROWS, 2 * ff_pad, _pad_halves_body))
        yb = _pool_mixer(up, w_pool[l].astype(BF16), pool_scale[l], seq, SIDE_TILE)

        mem_rows = batch * mem_len
        mem_n = _rmsnorm(mem.reshape(mem_rows, d_model), g_mem[l], NORM_ROWS)
        w_mkv_b = w_mkv[l].astype(BF16)
        km = _proj_norm(mem_n, w_mkv_b, g_km[l], 0, mem_width, mem_rows, tn, "proj_km")
        vm = _proj_plain(mem_n, w_mkv_b, mem_width, mem_width, BF16, mem_rows, tn, "proj_vm")
        yc = _mem_attn(qm, km, vm, batch, seq, mem_len, mem_width // mem_hd, SIDE_TILE)

        merged = _merge(ya, yb, yc, w_a[l].astype(BF16), w_b[l].astype(BF16), w_c[l].astype(BF16),
                        gl, tm, tnh)
        xf = _proj_residual(merged, w_o[l].astype(BF16), xf, tm, tnh)

        h2 = _rmsnorm(xf, g_ffn_norm[l], NORM_ROWS)
        act = _ffn_up(h2, w_up_b, _pad_halves(conv_w[l], d_ff, ff_pad),
                      _pad_halves(conv_b[l], d_ff, ff_pad), seq, tm, tnh)
        xf = _ffn_down(act, w_down_b, xf, tm, tn, ff_pad // FFN_DOWN_K_STEPS)

    return xf.reshape(batch, seq, d_model)
```

```python
import functools
import math
from typing import Callable, NamedTuple

import jax
import jax.numpy as jnp
from jax import lax
from jax.experimental import pallas as pl
from jax.experimental.pallas import tpu as pltpu

F32 = jnp.float32
BF16 = jnp.bfloat16

EPS = 1e-6
ROPE_THETA = 500000.0
CHUNK = 64
POOL_WINDOWS = (2, 4, 8, 16)
CONV_W = 3

LANES = 128
SUBLANES = 8
SUM_ROWS = 2 * SUBLANES
VMEM_LIMIT_BYTES = 56 * 1024 * 1024
LOG2E = 1.4426950408889634

ROW_TILE = 1024
COL_TILE = 1024
HALF_COL_TILE = 512
ROW_CHUNK = 256
NORM_ROWS = 512
ATTN_TILE = 1024
SIDE_TILE = 512
FF_ALIGN = 1024
FFN_DOWN_K_STEPS = 4
W_IN_RIDER_ROWS = 256
W_DOWN_RIDER_ROWS = 128
W_UP_RIDER_ROWS = 32


def _params(*sem):
    return pltpu.CompilerParams(dimension_semantics=sem, vmem_limit_bytes=VMEM_LIMIT_BYTES)


def _group_rmsnorm(x, g):
    ms = jnp.mean(x * x, axis=-1, keepdims=True)
    return x * lax.rsqrt(ms + EPS) * g


class _Rider(NamedTuple):
    src: jax.Array
    in_block: tuple
    in_index: Callable
    out_shape: tuple
    out_block: tuple
    out_index: Callable
    nblocks: int
    body: Callable


def _rider_io(rider, step_of, nsteps):
    assert rider.nblocks <= nsteps, "the host call has too few grid steps for this rider"
    block_of = lambda *g: jnp.minimum(step_of(*g), rider.nblocks - 1)
    in_spec = pl.BlockSpec(rider.in_block, lambda *g: rider.in_index(block_of(*g)))
    out_spec = pl.BlockSpec(rider.out_block, lambda *g: rider.out_index(block_of(*g)))
    return in_spec, out_spec, jax.ShapeDtypeStruct(rider.out_shape, BF16)


def _run_rider(nblocks, body, step, src_ref, dst_ref):
    @pl.when(step < nblocks)
    def _():
        body(src_ref, dst_ref)


def _cast_body(src_ref, dst_ref):
    dst_ref[...] = src_ref[...].astype(dst_ref.dtype)


def _pad_halves_body(src_ref, dst_ref):
    ff = src_ref.shape[1] // 2
    ff_pad = dst_ref.shape[1] // 2
    zeros = jnp.zeros((dst_ref.shape[0], ff_pad - ff), dst_ref.dtype)
    for half in range(2):
        dst_ref[:, half * ff_pad:half * ff_pad + ff] = src_ref[:, half * ff:(half + 1) * ff].astype(dst_ref.dtype)
        dst_ref[:, half * ff_pad + ff:(half + 1) * ff_pad] = zeros


def _row_block_rider(src, tr, out_cols, body):
    rows, cols = src.shape
    assert rows % tr == 0
    return _Rider(src, (tr, cols), lambda n: (n, 0), (rows, out_cols), (tr, out_cols), lambda n: (n, 0),
                  rows // tr, body)


def _rmsnorm_kernel(x_ref, g_ref, o_ref):
    o_ref[...] = _group_rmsnorm(x_ref[...], g_ref[...]).astype(o_ref.dtype)


def _rmsnorm(x, g, tm):
    m, d = x.shape
    return pl.pallas_call(
        _rmsnorm_kernel,
        grid=(m // tm,),
        in_specs=[pl.BlockSpec((tm, d), lambda i: (i, 0)), pl.BlockSpec((1, d), lambda i: (0, 0))],
        out_specs=pl.BlockSpec((tm, d), lambda i: (i, 0)),
        out_shape=jax.ShapeDtypeStruct((m, d), BF16),
        compiler_params=_params("parallel"),
        name="rmsnorm",
    )(x, g.reshape(1, d))


def _proj_plain(x, w, col0, ncols, out_dtype, tm, tn, name, rider=None):
    m, k = x.shape
    j0 = col0 // tn
    nj = ncols // tn
    rider_args = None if rider is None else (rider.nblocks, rider.body)

    def kernel(x_ref, w_ref, *refs):
        o_ref = refs[-1] if rider_args is None else refs[1]
        o_ref[...] = jnp.dot(x_ref[...], w_ref[...], preferred_element_type=F32).astype(o_ref.dtype)
        if rider_args is not None:
            _run_rider(*rider_args, pl.program_id(0) * nj + pl.program_id(1), refs[0], refs[2])

    in_specs = [pl.BlockSpec((tm, k), lambda i, j: (i, 0)),
                pl.BlockSpec((k, tn), lambda i, j: (0, j0 + j))]
    out_specs = [pl.BlockSpec((tm, tn), lambda i, j: (i, j))]
    out_shape = [jax.ShapeDtypeStruct((m, ncols), out_dtype)]
    args = [x, w]
    if rider is not None:
        r_in, r_out, r_shape = _rider_io(rider, lambda i, j: i * nj + j, (m // tm) * nj)
        in_specs.append(r_in)
        out_specs.append(r_out)
        out_shape.append(r_shape)
        args.append(rider.src)
    sem = ("parallel", "parallel") if rider is None else ("arbitrary", "arbitrary")
    outs = pl.pallas_call(
        kernel,
        grid=(m // tm, nj),
        in_specs=in_specs,
        out_specs=out_specs,
        out_shape=out_shape,
        compiler_params=_params(*sem),
        name=name,
    )(*args)
    return outs[0] if rider is None else tuple(outs)


def _proj_norm_kernel(x_ref, w_ref, g_ref, o_ref, *, gdim):
    acc = jnp.dot(x_ref[...], w_ref[...], preferred_element_type=F32)
    for c in range(acc.shape[1] // gdim):
        sl = slice(c * gdim, (c + 1) * gdim)
        o_ref[:, sl] = _group_rmsnorm(acc[:, sl], g_ref[...]).astype(o_ref.dtype)


def _proj_norm(x, w, g, col0, ncols, tm, tn, name):
    m, k = x.shape
    gdim = g.shape[0]
    j0 = col0 // tn
    return pl.pallas_call(
        functools.partial(_proj_norm_kernel, gdim=gdim),
        grid=(m // tm, ncols // tn),
        in_specs=[pl.BlockSpec((tm, k), lambda i, j: (i, 0)),
                  pl.BlockSpec((k, tn), lambda i, j: (0, j0 + j)),
                  pl.BlockSpec((1, gdim), lambda i, j: (0, 0))],
        out_specs=pl.BlockSpec((tm, tn), lambda i, j: (i, j)),
        out_shape=jax.ShapeDtypeStruct((m, ncols), BF16),
        compiler_params=_params("parallel", "parallel"),
        name=name,
    )(x, w, g.reshape(1, gdim))


def _proj_qk(x, w, g_cols, cos_t, sa_t, sb_t, seq, rope_dim, tm, tn, rider):
    m, k = x.shape
    ncols = g_cols.shape[0]
    nseq = seq // tm
    nj = ncols // tn
    half = rope_dim // 2
    rider_args = (rider.nblocks, rider.body)

    def kernel(x_ref, w_ref, g_ref, cos_ref, sa_ref, sb_ref, rsrc_ref, o_ref, rdst_ref):
        for r in range(tm // ROW_CHUNK):
            rows = slice(r * ROW_CHUNK, (r + 1) * ROW_CHUNK)
            acc = jnp.dot(x_ref[rows, :], w_ref[...], preferred_element_type=F32)
            for c in range(tn // LANES):
                sl = slice(c * LANES, (c + 1) * LANES)
                y = _group_rmsnorm(acc[:, sl], g_ref[:, sl])
                y = (y * cos_ref[rows, :] + pltpu.roll(y, LANES - half, 1) * sa_ref[rows, :]
                     + pltpu.roll(y, half, 1) * sb_ref[rows, :])
                o_ref[rows, sl] = y.astype(o_ref.dtype)
        _run_rider(*rider_args, pl.program_id(0) * nj + pl.program_id(1), rsrc_ref, rdst_ref)

    tab = pl.BlockSpec((tm, LANES), lambda i, j: (i % nseq, 0))
    r_in, r_out, r_shape = _rider_io(rider, lambda i, j: i * nj + j, (m // tm) * nj)
    return pl.pallas_call(
        kernel,
        grid=(m // tm, nj),
        in_specs=[pl.BlockSpec((tm, k), lambda i, j: (i, 0)),
                  pl.BlockSpec((k, tn), lambda i, j: (0, j)),
                  pl.BlockSpec((1, tn), lambda i, j: (0, j)),
                  tab, tab, tab, r_in],
        out_specs=[pl.BlockSpec((tm, tn), lambda i, j: (i, j)), r_out],
        out_shape=[jax.ShapeDtypeStruct((m, ncols), BF16), r_shape],
        compiler_params=_params("arbitrary", "arbitrary"),
        name="proj_qk",
    )(x, w, g_cols.reshape(1, ncols), cos_t, sa_t, sb_t, rider.src)


def _diff_attn_kernel(q_ref, k_ref, v_ref, lam_ref, gs_ref, rsrc_ref, o_ref, rdst_ref,
                      vt_scr, s_scr, p_scr, a_scr, mx_scr, m_scr, acc_scr,
                      *, tq, tk, dk, lambda_init, rider_args):
    i = pl.program_id(2)
    dv = 2 * dk
    neg = -1e30
    step =(pl.program_id(0) * pl.num_programs(1) + pl.program_id(1)) * pl.num_programs(2) + i
    _run_rider(*rider_args, step, rsrc_ref, rdst_ref)

    @pl.when(i == 0)
    def _():
        for blk in range(vt_scr.shape[0]):
            vt_scr[blk, 0:dv, :] = v_ref[blk * tk:(blk + 1) * tk, :].T
            vt_scr[blk, dv:, :] = jnp.ones((SUM_ROWS, tk), BF16)

    m_scr[...] = jnp.full(m_scr.shape, neg, F32)
    acc_scr[...] = jnp.zeros(acc_scr.shape, F32)

    def scores(blk, slot, diag=None, qlo=0):
        k = k_ref[pl.ds(pl.multiple_of(blk * tk, tk), tk), :]
        if diag is not None:
            key_chunk = (diag * tk + lax.broadcasted_iota(jnp.int32, (tk, tq - qlo), 0)) // CHUNK
            q_chunk = (qlo + lax.broadcasted_iota(jnp.int32, (tk, tq - qlo), 1)) // CHUNK
            allowed = key_chunk <= q_chunk
        for c in range(2):
            sl = slice(c * dk, (c + 1) * dk)
            s = lax.dot_general(k[:, sl], q_ref[qlo:, sl], (((1,), (1,)), ((), ())),
                                preferred_element_type=F32)
            if diag is not None:
                s = jnp.where(allowed, s, neg)
            s_scr[slot, c, :, qlo:] = s
            mx_scr[slot, c, :, qlo:] = jnp.max(s, axis=0, keepdims=True)

    def softmax(slot, qlo=0):
        hk = tk // 4
        for c in range(2):
            m_prev = m_scr[c, :, qlo:]
            m_new = jnp.maximum(m_prev, mx_scr[slot, c, :, qlo:])
            for h in range(4):
                rows = slice(h * hk, (h + 1) * hk)
                p_scr[slot, c, rows, qlo:] = jnp.exp2(s_scr[slot, c, rows, qlo:] - m_new).astype(BF16)
            a_scr[slot, c, :, qlo:] = jnp.exp2(m_prev - m_new)
            m_scr[c, :, qlo:] = m_new

    def values(slot, blk, qlo=0):
        vt = vt_scr[blk]
        for c in range(2):
            acc_scr[c, :, qlo:] = (a_scr[slot, c, :, qlo:] * acc_scr[c, :, qlo:]
                                   + jnp.dot(vt, p_scr[slot, c, :, qlo:], preferred_element_type=F32))

    scores(2 * i + 1, 0, diag=1, qlo=tk)
    scores(2 * i, 1, diag=0)
    softmax(0, qlo=tk)
    scores(0, 0)
    softmax(1)
    values(0, 2 * i + 1, tk)

    def pair(j, last):
        scores(2 * j - 1, 1)
        softmax(0)
        values(1, jnp.where(j == 1, 2 * i, 2 * j - 3))
        if not last:
            scores(2 * j, 0)
        softmax(1)
        values(0, 2 * j - 2)

    def body(j, carry):
        pair(j, False)
        return carry

    lax.fori_loop(1, i, body, 0)

    @pl.when(i >= 1)
    def _():
        pair(i, True)

    values(1, jnp.where(i == 0, 0, 2 * i - 1))

    lam_v = lam_ref[...]
    lam = (jnp.exp(jnp.sum(lam_v[0:1] * lam_v[1:2], axis=-1, keepdims=True))
           - jnp.exp(jnp.sum(lam_v[2:3] * lam_v[3:4], axis=-1, keepdims=True)) + lambda_init)
    o = (acc_scr[0, 0:dv, :] * (1.0 / acc_scr[0, dv:dv + 1, :])
         - acc_scr[1, 0:dv, :] * (lam / acc_scr[1, dv:dv + 1, :]))
    ms = jnp.mean(o * o, axis=0, keepdims=True)
    y = o * lax.rsqrt(ms + EPS) * gs_ref[...] * (1.0 - lambda_init)
    o_ref[...] = y.T.astype(o_ref.dtype)


def _diff_attn(qk, v, lam_vecs, g_subln, batch, seq, heads, dk, lambda_init, tq, rider):
    dv = 2 * dk
    tk = tq // 2
    nq = seq // tq
    kernel = functools.partial(_diff_attn_kernel, tq=tq, tk=tk, dk=dk, lambda_init=lambda_init,
                               rider_args=(rider.nblocks, rider.body))
    r_in, r_out, r_shape = _rider_io(rider, lambda b, h, i: (b * heads + h) * nq + i, batch * heads * nq)
    return pl.pallas_call(
        kernel,
        grid=(batch, heads, nq),
        in_specs=[pl.BlockSpec((tq, dv), lambda b, h, i: (b * nq + i, h)),
                  pl.BlockSpec((seq, dv), lambda b, h, i: (b, heads + h)),
                  pl.BlockSpec((seq, dv), lambda b, h, i: (b, h)),
                  pl.BlockSpec((4, dk), lambda b, h, i: (0, 0)),
                  pl.BlockSpec((dv, 1), lambda b, h, i: (0, 0)), r_in],
        out_specs=[pl.BlockSpec((tq, dv), lambda b, h, i: (b * nq + i, h)), r_out],
        out_shape=[jax.ShapeDtypeStruct((batch * seq, heads * dv), BF16), r_shape],
        scratch_shapes=[pltpu.VMEM((seq // tk, dv + SUM_ROWS, tk), BF16),
                        pltpu.VMEM((2, 2, tk, tq), F32), pltpu.VMEM((2, 2, tk, tq), BF16),
                        pltpu.VMEM((2, 2, 1, tq), F32), pltpu.VMEM((2, 2, 1, tq), F32),
                        pltpu.VMEM((2, 1, tq), F32), pltpu.VMEM((2, dv + SUM_ROWS, tq), F32)],
        compiler_params=_params("arbitrary", "arbitrary", "arbitrary"),
        name="diff_attn",
    )(qk, qk, v, lam_vecs, g_subln.reshape(dv, 1), rider.src)


def _pool_kernel(up_ref, prev_ref, wp_ref, ps_ref, o_ref, ext_scr, *, tm, nseq, halo, gdim):
    i = pl.program_id(0)
    ti = i % nseq
    ext_scr[halo:halo + tm, :] = up_ref[...]
    ext_scr[0:halo, :] = jnp.where(ti == 0, 0.0, prev_ref[...])
    t1 = (ti * tm + 1 + lax.broadcasted_iota(jnp.int32, (tm, 1), 0)).astype(F32)
    for g, win in enumerate(POOL_WINDOWS):
        sl = slice(g * gdim, (g + 1) * gdim)
        cur = ext_scr[halo:halo + tm, sl]
        tot = cur
        for d in range(1, win):
            tot = tot + ext_scr[halo - d:halo - d + tm, sl]
        y = tot / jnp.minimum(t1, float(win)) - cur
        z = jnp.dot(y.astype(BF16), wp_ref[g], preferred_element_type=F32) * ps_ref[g:g + 1, :]
        o_ref[:, sl] = z.astype(o_ref.dtype)


def _pool_mixer(up, w_pool, pool_scale, seq, tm):
    m, width = up.shape
    groups, gdim, _ = w_pool.shape
    halo = 2 * SUBLANES
    assert max(POOL_WINDOWS) <= halo
    nseq = seq // tm
    per = tm // halo
    kernel = functools.partial(_pool_kernel, tm=tm, nseq=nseq, halo=halo, gdim=gdim)
    return pl.pallas_call(
        kernel,
        grid=(m // tm,),
        in_specs=[pl.BlockSpec((tm, width), lambda i: (i, 0)),
                  pl.BlockSpec((halo, width), lambda i: (jnp.maximum(i * per - 1, 0), 0)),
                  pl.BlockSpec((groups, gdim, gdim), lambda i: (0, 0, 0)),
                  pl.BlockSpec((groups, gdim), lambda i: (0, 0))],
        out_specs=pl.BlockSpec((tm, width), lambda i: (i, 0)),
        out_shape=jax.ShapeDtypeStruct((m, width), BF16),
        scratch_shapes=[pltpu.VMEM((tm + halo, width), F32)],
        compiler_params=_params("parallel"),
        name="pool_mixer",
    )(up, up, w_pool, pool_scale)


def _mem_attn_kernel(q_ref, k_ref, v_ref, o_ref, *, heads, hd):
    scale = hd ** -0.5
    for h in range(heads):
        sl = slice(h * hd, (h + 1) * hd)
        s = lax.dot_general(q_ref[:, sl], k_ref[:, sl], (((1,), (1,)), ((), ())),
                            preferred_element_type=F32) * scale
        p = jnp.exp(s - jnp.max(s, axis=-1, keepdims=True))
        l = jnp.sum(p, axis=-1, keepdims=True)
        o = jnp.dot(p.astype(BF16), v_ref[:, sl], preferred_element_type=F32) / l
        o_ref[:, sl] = o.astype(o_ref.dtype)


def _mem_attn(qm, km, vm, batch, seq, mem_len, heads, tq):
    m, width = qm.shape
    nq = seq // tq
    kernel = functools.partial(_mem_attn_kernel, heads=heads, hd=width // heads)
    return pl.pallas_call(
        kernel,
        grid=(batch, nq),
        in_specs=[pl.BlockSpec((tq, width), lambda b, i: (b * nq + i, 0)),
                  pl.BlockSpec((mem_len, width), lambda b, i: (b, 0)),
                  pl.BlockSpec((mem_len, width), lambda b, i: (b, 0))],
        out_specs=pl.BlockSpec((tq, width), lambda b, i: (b * nq + i, 0)),
        out_shape=jax.ShapeDtypeStruct((m, width), BF16),
        compiler_params=_params("parallel", "parallel"),
        name="mem_attn",
    )(qm, km, vm)


def _merge_kernel(ya_ref, yb_ref, yc_ref, wa_ref, wb_ref, wc_ref, ga_ref, gb_ref, gc_ref, o_ref):
    out = jax.nn.sigmoid(ga_ref[...]) * jnp.dot(ya_ref[...], wa_ref[...], preferred_element_type=F32)
    out = out + jax.nn.sigmoid(gb_ref[...]) * jnp.dot(yb_ref[...], wb_ref[...],
                                                      preferred_element_type=F32)
    out = out + jax.nn.sigmoid(gc_ref[...]) * jnp.dot(yc_ref[...], wc_ref[...],
                                                      preferred_element_type=F32)
    o_ref[...] = out.astype(o_ref.dtype)


def _merge(ya, yb, yc, w_a, w_b, w_c, gl, tm, tn):
    m = ya.shape[0]
    d = w_a.shape[1]
    nj = d // tn
    row = lambda a: pl.BlockSpec((tm, a.shape[1]), lambda i, j: (i, 0))
    col = lambda w: pl.BlockSpec((w.shape[0], tn), lambda i, j: (0, j))
    gate = lambda b: pl.BlockSpec((tm, tn), lambda i, j: (i, b * nj + j))
    return pl.pallas_call(
        _merge_kernel,
        grid=(m // tm, nj),
        in_specs=[row(ya), row(yb), row(yc), col(w_a), col(w_b), col(w_c), gate(0), gate(1), gate(2)],
        out_specs=pl.BlockSpec((tm, tn), lambda i, j: (i, j)),
        out_shape=jax.ShapeDtypeStruct((m, d), BF16),
        compiler_params=_params("parallel", "parallel"),
        name="merge",
    )(ya, yb, yc, w_a, w_b, w_c, gl, gl, gl)


def _proj_residual_kernel(a_ref, w_ref, r_ref, o_ref):
    o_ref[...] = r_ref[...] + jnp.dot(a_ref[...], w_ref[...], preferred_element_type=F32)


def _proj_residual(a, w, res, tm, tn):
    m, k = a.shape
    n = w.shape[1]
    return pl.pallas_call(
        _proj_residual_kernel,
        grid=(m // tm, n // tn),
        in_specs=[pl.BlockSpec((tm, k), lambda i, j: (i, 0)),
                  pl.BlockSpec((k, tn), lambda i, j: (0, j)),
                  pl.BlockSpec((tm, tn), lambda i, j: (i, j))],
        out_specs=pl.BlockSpec((tm, tn), lambda i, j: (i, j)),
        out_shape=jax.ShapeDtypeStruct((m, n), F32),
        compiler_params=_params("parallel", "parallel"),
        name="proj_residual",
    )(a, w, res)


def _ffn_up_kernel(h_ref, wg_ref, wv_ref, cwg_ref, cwv_ref, cbg_ref, cbv_ref, o_ref, yg_scr, yv_scr,
                   *, tm, nseq):
    i = pl.program_id(1)
    pad = SUBLANES

    @pl.when(i % nseq == 0)
    def _():
        yg_scr[0:pad, :] = jnp.zeros((pad, yg_scr.shape[1]), F32)
        yv_scr[0:pad, :] = jnp.zeros((pad, yv_scr.shape[1]), F32)

    def conv(r, y_scr, w_ref, cw_ref, cb_ref):
        lo = pad + r * ROW_CHUNK
        y_scr[lo:lo + ROW_CHUNK, :] = jnp.dot(h_ref[r * ROW_CHUNK:(r + 1) * ROW_CHUNK, :], w_ref[...],
                                              preferred_element_type=F32)
        u = cb_ref[...] + y_scr[lo:lo + ROW_CHUNK, :] * cw_ref[CONV_W - 1:CONV_W, :]
        for d in range(1, CONV_W):
            u = u + y_scr[lo - d:lo - d + ROW_CHUNK, :] * cw_ref[CONV_W - 1 - d:CONV_W - d, :]
        return u

    for r in range(tm // ROW_CHUNK):
        gate = conv(r, yg_scr, wg_ref, cwg_ref, cbg_ref)
        val = conv(r, yv_scr, wv_ref, cwv_ref, cbv_ref)
        o_ref[r * ROW_CHUNK:(r + 1) * ROW_CHUNK, :] = (gate * jax.nn.sigmoid(gate) * val).astype(o_ref.dtype)
    yg_scr[0:pad, :] = yg_scr[tm:tm + pad, :]
    yv_scr[0:pad, :] = yv_scr[tm:tm + pad, :]


def _ffn_up(h, w_up, conv_w, conv_b, seq, tm, tn):
    m, k = h.shape
    ff = w_up.shape[1] // 2
    nj = ff // tn
    nseq = seq // tm
    kernel = functools.partial(_ffn_up_kernel, tm=tm, nseq=nseq)
    wspec = lambda half: pl.BlockSpec((k, tn), lambda j, i: (0, half * nj + j))
    cwspec = lambda half: pl.BlockSpec((CONV_W, tn), lambda j, i: (0, half * nj + j))
    cbspec = lambda half: pl.BlockSpec((1, tn), lambda j, i: (0, half * nj + j))
    scratch = pltpu.VMEM((tm + SUBLANES, tn), F32)
    return pl.pallas_call(
        kernel,
        grid=(nj, m // tm),
        in_specs=[pl.BlockSpec((tm, k), lambda j, i: (i, 0)),
                  wspec(0), wspec(1), cwspec(0), cwspec(1), cbspec(0), cbspec(1)],
        out_specs=pl.BlockSpec((tm, tn), lambda j, i: (i, j)),
        out_shape=jax.ShapeDtypeStruct((m, ff), BF16),
        scratch_shapes=[scratch, scratch],
        compiler_params=_params("arbitrary", "arbitrary"),
        name="ffn_up",
    )(h, w_up, w_up, conv_w, conv_w, conv_b.reshape(1, -1), conv_b.reshape(1, -1))


def _ffn_down_kernel(a_ref, w_ref, r_ref, o_ref, acc_scr, *, rem):
    kk = pl.program_id(2)
    last = pl.num_programs(2) - 1

    @pl.when(kk == 0)
    def _():
        acc_scr[...] = r_ref[...]

    @pl.when(kk < last)
    def _():
        acc_scr[...] += jnp.dot(a_ref[...], w_ref[...], preferred_element_type=F32)

    @pl.when(kk == last)
    def _():
        o_ref[...] = acc_scr[...] + jnp.dot(a_ref[:, :rem], w_ref[:rem, :], preferred_element_type=F32)


def _ffn_down(a, w, res, tm, tn, tk):
    m = a.shape[0]
    k, n = w.shape
    nk = pl.cdiv(k, tk)
    rem = k - (nk - 1) * tk
    assert tk % LANES == 0 and rem % LANES == 0 and a.shape[1] >= nk * tk
    return pl.pallas_call(
        functools.partial(_ffn_down_kernel, rem=rem),
        grid=(m // tm, n // tn, nk),
        in_specs=[pl.BlockSpec((tm, tk), lambda i, j, kk: (i, kk)),
                  pl.BlockSpec((tk, tn), lambda i, j, kk: (kk, j)),
                  pl.BlockSpec((tm, tn), lambda i, j, kk: (i, j))],
        out_specs=pl.BlockSpec((tm, tn), lambda i, j, kk: (i, j)),
        out_shape=jax.ShapeDtypeStruct((m, n), F32),
        scratch_shapes=[pltpu.VMEM((tm, tn), F32)],
        compiler_params=_params("parallel", "parallel", "arbitrary"),
        name="ffn_down",
    )(a, w, res)


def _rope_tables(seq, rope_dim):
    half = rope_dim // 2
    pos = jnp.arange(seq, dtype=F32)
    inv = ROPE_THETA ** (-jnp.arange(0, rope_dim, 2, dtype=F32) / rope_dim)
    ang = pos[:, None] * inv[None, :]
    cos, sin = jnp.cos(ang), jnp.sin(ang)
    ones = jnp.ones((seq, LANES - rope_dim), F32)
    zeros = jnp.zeros((seq, LANES - rope_dim), F32)
    zh = jnp.zeros((seq, half), F32)
    cos_t = jnp.concatenate([cos, cos, ones], axis=1)
    sa_t = jnp.concatenate([-sin, zh, zeros], axis=1)
    sb_t = jnp.concatenate([zh, sin, zeros], axis=1)
    return cos_t, sa_t, sb_t


def _pad_halves(a, ff, ff_pad):
    pad = [(0, 0)] * (a.ndim - 1) + [(0, ff_pad - ff)]
    return jnp.concatenate([jnp.pad(a[..., :ff], pad), jnp.pad(a[..., ff:], pad)], axis=-1)


def kernel(x, mem, g_attn_norm, w_in, g_qa, g_ka, lam_q1, lam_k1, lam_q2, lam_k2, g_subln, w_pool,
           pool_scale, g_mem, w_mkv, g_qm, g_km, w_a, w_b, w_c, w_o, g_ffn_norm, w_up, conv_w, conv_b,
           w_down):
    batch, seq, d_model = x.shape
    mem_len = mem.shape[1]
    depth = w_in.shape[0]
    dk = g_qa.shape[1]
    dv = g_subln.shape[1]
    heads = w_a.shape[1] // dv
    qk_width = heads * 2 * dk
    pool_width = w_b.shape[1]
    mem_width = w_c.shape[1]
    mem_hd = g_qm.shape[1]
    rope_dim = dk // 4
    d_ff = w_down.shape[1]
    ff_pad = -(-d_ff // FF_ALIGN) * FF_ALIGN
    assert dk == LANES and dv == 2 * dk
    tm, tn, tnh = ROW_TILE, COL_TILE, HALF_COL_TILE

    m = batch * seq
    xf = x.reshape(m, d_model)
    cos_t, sa_t, sb_t = _rope_tables(seq, rope_dim)
    c_v = 2 * qk_width
    c_up = c_v + heads * dv
    c_qm = c_up + pool_width
    c_gl = c_qm + mem_width

    for l in range(depth):
        lambda_init = 0.8 - 0.6 * math.exp(-0.3 * l)
        g_qk = jnp.concatenate([jnp.tile(g_qa[l] * (dk ** -0.5 * LOG2E), qk_width // dk),
                                jnp.tile(g_ka[l], qk_width // dk)])
        lam_vecs = jnp.stack([lam_q1[l], lam_k1[l], lam_q2[l], lam_k2[l]]).astype(F32)

        w_qk_b = w_in[l][:, :c_v].astype(BF16)
        nrest = (w_in.shape[2] - c_v) // c_v
        rr = W_IN_RIDER_ROWS
        rest_rider = _Rider(w_in[l], (rr, c_v), lambda n: (n // nrest, 1 + n % nrest),
                            (d_model, nrest * c_v), (rr, c_v), lambda n: (n // nrest, n % nrest),
                            (d_model // rr) * nrest, _cast_body)

        h = _rmsnorm(xf, g_attn_norm[l], NORM_ROWS)
        qk, w_rest_b = _proj_qk(h, w_qk_b, g_qk, cos_t, sa_t, sb_t, seq, rope_dim, tm, tn, rest_rider)
        v = _proj_plain(h, w_rest_b, 0, heads * dv, BF16, tm, tn, "proj_v")
        up = _proj_plain(h, w_rest_b, c_up - c_v, pool_width, F32, tm, tn, "proj_up")
        qm = _proj_norm(h, w_rest_b, g_qm[l], c_qm - c_v, mem_width, tm, tn, "proj_qm")
        gl, w_down_b = _proj_plain(h, w_rest_b, c_gl - c_v, 3 * d_model, F32, tm, tn, "proj_gates",
                                   rider=_row_block_rider(w_down[l], W_DOWN_RIDER_ROWS, d_model, _cast_body))

        ya, w_up_b = _diff_attn(qk, v, lam_vecs, g_subln[l], batch, seq, heads, dk, lambda_init, ATTN_TILE,
                                _row_block_rider(w_up[l], W_UP_RIDER_ROWS, 2 * ff_pad, _pad_halves_body))
        yb = _pool_mixer(up, w_pool[l].astype(BF16), pool_scale[l], seq, SIDE_TILE)

        mem_rows = batch * mem_len
        mem_n = _rmsnorm(mem.reshape(mem_rows, d_model), g_mem[l], NORM_ROWS)
        w_mkv_b = w_mkv[l].astype(BF16)
        km = _proj_norm(mem_n, w_mkv_b, g_km[l], 0, mem_width, mem_rows, tn, "proj_km")
        vm = _proj_plain(mem_n, w_mkv_b, mem_width, mem_width, BF16, mem_rows, tn, "proj_vm")
        yc = _mem_attn(qm, km, vm, batch, seq, mem_len, mem_width // mem_hd, SIDE_TILE)

        merged = _merge(ya, yb, yc, w_a[l].astype(BF16), w_b[l].astype(BF16), w_c[l].astype(BF16),
                        gl, tm, tnh)
        xf = _proj_residual(merged, w_o[l].astype(BF16), xf, tm, tnh)

        h2 = _rmsnorm(xf, g_ffn_norm[l], NORM_ROWS)
        act = _ffn_up(h2, w_up_b, _pad_halves(conv_w[l], d_ff, ff_pad),
                      _pad_halves(conv_b[l], d_ff, ff_pad), seq, tm, tnh)
        xf = _ffn_down(act, w_down_b, xf, tm, tn, ff_pad // FFN_DOWN_K_STEPS)

    return xf.reshape(batch, seq, d_model)
```

```python
import functools
import math
from typing import Callable, NamedTuple

import jax
import jax.numpy as jnp
from jax import lax
from jax.experimental import pallas as pl
from jax.experimental.pallas import tpu as pltpu

F32 = jnp.float32
BF16 = jnp.bfloat16

EPS = 1e-6
ROPE_THETA = 500000.0
CHUNK = 64
POOL_WINDOWS = (2, 4, 8, 16)
CONV_W = 3

LANES = 128
SUBLANES = 8
SUM_ROWS = 2 * SUBLANES
VMEM_LIMIT_BYTES = 56 * 1024 * 1024
LOG2E = 1.4426950408889634

ROW_TILE = 1024
COL_TILE = 1024
HALF_COL_TILE = 512
ROW_CHUNK = 256
NORM_ROWS = 512
ATTN_TILE = 1024
SIDE_TILE = 512
FF_ALIGN = 1024
FFN_DOWN_K_STEPS = 4
W_IN_RIDER_ROWS = 256
W_DOWN_RIDER_ROWS = 128
W_UP_RIDER_ROWS = 32


def _params(*sem):
    return pltpu.CompilerParams(dimension_semantics=sem, vmem_limit_bytes=VMEM_LIMIT_BYTES)


def _group_rmsnorm(x, g):
    ms = jnp.mean(x * x, axis=-1, keepdims=True)
    return x * lax.rsqrt(ms + EPS) * g


class _Rider(NamedTuple):
    src: jax.Array
    in_block: tuple
    in_index: Callable
    out_shape: tuple
    out_block: tuple
    out_index: Callable
    nblocks: int
    body: Callable


def _rider_io(rider, step_of, nsteps):
    assert rider.nblocks <= nsteps, "the host call has too few grid steps for this rider"
    block_of = lambda *g: jnp.minimum(step_of(*g), rider.nblocks - 1)
    in_spec = pl.BlockSpec(rider.in_block, lambda *g: rider.in_index(block_of(*g)))
    out_spec = pl.BlockSpec(rider.out_block, lambda *g: rider.out_index(block_of(*g)))
    return in_spec, out_spec, jax.ShapeDtypeStruct(rider.out_shape, BF16)


def _run_rider(nblocks, body, step, src_ref, dst_ref):
    @pl.when(step < nblocks)
    def _():
        body(src_ref, dst_ref)


def _cast_body(src_ref, dst_ref):
    dst_ref[...] = src_ref[...].astype(dst_ref.dtype)


def _pad_halves_body(src_ref, dst_ref):
    ff = src_ref.shape[1] // 2
    ff_pad = dst_ref.shape[1] // 2
    zeros = jnp.zeros((dst_ref.shape[0], ff_pad - ff), dst_ref.dtype)
    for half in range(2):
        dst_ref[:, half * ff_pad:half * ff_pad + ff] = src_ref[:, half * ff:(half + 1) * ff].astype(dst_ref.dtype)
        dst_ref[:, half * ff_pad + ff:(half + 1) * ff_pad] = zeros


def _row_block_rider(src, tr, out_cols, body):
    rows, cols = src.shape
    assert rows % tr == 0
    return _Rider(src, (tr, cols), lambda n: (n, 0), (rows, out_cols), (tr, out_cols), lambda n: (n, 0),
                  rows // tr, body)


def _rmsnorm_kernel(x_ref, g_ref, o_ref):
    o_ref[...] = _group_rmsnorm(x_ref[...], g_ref[...]).astype(o_ref.dtype)


def _rmsnorm(x, g, tm):
    m, d = x.shape
    return pl.pallas_call(
        _rmsnorm_kernel,
        grid=(m // tm,),
        in_specs=[pl.BlockSpec((tm, d), lambda i: (i, 0)), pl.BlockSpec((1, d), lambda i: (0, 0))],
        out_specs=pl.BlockSpec((tm, d), lambda i: (i, 0)),
        out_shape=jax.ShapeDtypeStruct((m, d), BF16),
        compiler_params=_params("parallel"),
        name="rmsnorm",
    )(x, g.reshape(1, d))


def _proj_plain(x, w, col0, ncols, out_dtype, tm, tn, name, rider=None):
    m, k = x.shape
    j0 = col0 // tn
    nj = ncols // tn
    rider_args = None if rider is None else (rider.nblocks, rider.body)

    def kernel(x_ref, w_ref, *refs):
        o_ref = refs[-1] if rider_args is None else refs[1]
        o_ref[...] = jnp.dot(x_ref[...], w_ref[...], preferred_element_type=F32).astype(o_ref.dtype)
        if rider_args is not None:
            _run_rider(*rider_args, pl.program_id(0) * nj + pl.program_id(1), refs[0], refs[2])

    in_specs = [pl.BlockSpec((tm, k), lambda i, j: (i, 0)),
                pl.BlockSpec((k, tn), lambda i, j: (0, j0 + j))]
    out_specs = [pl.BlockSpec((tm, tn), lambda i, j: (i, j))]
    out_shape = [jax.ShapeDtypeStruct((m, ncols), out_dtype)]
    args = [x, w]
    if rider is not None:
        r_in, r_out, r_shape = _rider_io(rider, lambda i, j: i * nj + j, (m // tm) * nj)
        in_specs.append(r_in)
        out_specs.append(r_out)
        out_shape.append(r_shape)
        args.append(rider.src)
    sem = ("parallel", "parallel") if rider is None else ("arbitrary", "arbitrary")
    outs = pl.pallas_call(
        kernel,
        grid=(m // tm, nj),
        in_specs=in_specs,
        out_specs=out_specs,
        out_shape=out_shape,
        compiler_params=_params(*sem),
        name=name,
    )(*args)
    return outs[0] if rider is None else tuple(outs)


def _proj_norm_kernel(x_ref, w_ref, g_ref, o_ref, *, gdim):
    acc = jnp.dot(x_ref[...], w_ref[...], preferred_element_type=F32)
    for c in range(acc.shape[1] // gdim):
        sl = slice(c * gdim, (c + 1) * gdim)
        o_ref[:, sl] = _group_rmsnorm(acc[:, sl], g_ref[...]).astype(o_ref.dtype)


def _proj_norm(x, w, g, col0, ncols, tm, tn, name):
    m, k = x.shape
    gdim = g.shape[0]
    j0 = col0 // tn
    return pl.pallas_call(
        functools.partial(_proj_norm_kernel, gdim=gdim),
        grid=(m // tm, ncols // tn),
        in_specs=[pl.BlockSpec((tm, k), lambda i, j: (i, 0)),
                  pl.BlockSpec((k, tn), lambda i, j: (0, j0 + j)),
                  pl.BlockSpec((1, gdim), lambda i, j: (0, 0))],
        out_specs=pl.BlockSpec((tm, tn), lambda i, j: (i, j)),
        out_shape=jax.ShapeDtypeStruct((m, ncols), BF16),
        compiler_params=_params("parallel", "parallel"),
        name=name,
    )(x, w, g.reshape(1, gdim))


def _proj_qk(x, w, g_cols, cos_t, sa_t, sb_t, seq, rope_dim, tm, tn, rider):
    m, k = x.shape
    ncols = g_cols.shape[0]
    nseq = seq // tm
    nj = ncols // tn
    half = rope_dim // 2
    rider_args = (rider.nblocks, rider.body)

    def kernel(x_ref, w_ref, g_ref, cos_ref, sa_ref, sb_ref, rsrc_ref, o_ref, rdst_ref):
        for r in range(tm // ROW_CHUNK):
            rows = slice(r * ROW_CHUNK, (r + 1) * ROW_CHUNK)
            acc = jnp.dot(x_ref[rows, :], w_ref[...], preferred_element_type=F32)
            for c in range(tn // LANES):
                sl = slice(c * LANES, (c + 1) * LANES)
                y = _group_rmsnorm(acc[:, sl], g_ref[:, sl])
                y = (y * cos_ref[rows, :] + pltpu.roll(y, LANES - half, 1) * sa_ref[rows, :]
                     + pltpu.roll(y, half, 1) * sb_ref[rows, :])
                o_ref[rows, sl] = y.astype(o_ref.dtype)
        _run_rider(*rider_args, pl.program_id(0) * nj + pl.program_id(1), rsrc_ref, rdst_ref)

    tab = pl.BlockSpec((tm, LANES), lambda i, j: (i % nseq, 0))
    r_in, r_out, r_shape = _rider_io(rider, lambda i, j: i * nj + j, (m // tm) * nj)
    return pl.pallas_call(
        kernel,
        grid=(m // tm, nj),
        in_specs=[pl.BlockSpec((tm, k), lambda i, j: (i, 0)),
                  pl.BlockSpec((k, tn), lambda i, j: (0, j)),
                  pl.BlockSpec((1, tn), lambda i, j: (0, j)),
                  tab, tab, tab, r_in],
        out_specs=[pl.BlockSpec((tm, tn), lambda i, j: (i, j)), r_out],
        out_shape=[jax.ShapeDtypeStruct((m, ncols), BF16), r_shape],
        compiler_params=_params("arbitrary", "arbitrary"),
        name="proj_qk",
    )(x, w, g_cols.reshape(1, ncols), cos_t, sa_t, sb_t, rider.src)


def _diff_attn_kernel(q_ref, k_ref, v_ref, lam_ref, gs_ref, rsrc_ref, o_ref, rdst_ref,
                      vt_scr, s_scr, p_scr, a_scr, mx_scr, m_scr, acc_scr,
                      *, tq, tk, dk, lambda_init, rider_args):
    i = pl.program_id(2)
    dv = 2 * dk
    neg = -1e30
    step =(pl.program_id(0) * pl.num_programs(1) + pl.program_id(1)) * pl.num_programs(2) + i
    _run_rider(*rider_args, step, rsrc_ref, rdst_ref)

    @pl.when(i == 0)
    def _():
        for blk in range(vt_scr.shape[0]):
            vt_scr[blk, 0:dv, :] = v_ref[blk * tk:(blk + 1) * tk, :].T
            vt_scr[blk, dv:, :] = jnp.ones((SUM_ROWS, tk), BF16)

    m_scr[...] = jnp.full(m_scr.shape, neg, F32)
    acc_scr[...] = jnp.zeros(acc_scr.shape, F32)

    def scores(blk, slot, diag=None, qlo=0):
        k = k_ref[pl.ds(pl.multiple_of(blk * tk, tk), tk), :]
        if diag is not None:
            key_chunk = (diag * tk + lax.broadcasted_iota(jnp.int32, (tk, tq - qlo), 0)) // CHUNK
            q_chunk = (qlo + lax.broadcasted_iota(jnp.int32, (tk, tq - qlo), 1)) // CHUNK
            allowed = key_chunk <= q_chunk
        for c in range(2):
            sl = slice(c * dk, (c + 1) * dk)
            s = lax.dot_general(k[:, sl], q_ref[qlo:, sl], (((1,), (1,)), ((), ())),
                                preferred_element_type=F32)
            if diag is not None:
                s = jnp.where(allowed, s, neg)
            s_scr[slot, c, :, qlo:] = s
            mx_scr[slot, c, :, qlo:] = jnp.max(s, axis=0, keepdims=True)

    def softmax(slot, qlo=0):
        hk = tk // 4
        for c in range(2):
            m_prev = m_scr[c, :, qlo:]
            m_new = jnp.maximum(m_prev, mx_scr[slot, c, :, qlo:])
            for h in range(4):
                rows = slice(h * hk, (h + 1) * hk)
                p_scr[slot, c, rows, qlo:] = jnp.exp2(s_scr[slot, c, rows, qlo:] - m_new).astype(BF16)
            a_scr[slot, c, :, qlo:] = jnp.exp2(m_prev - m_new)
            m_scr[c, :, qlo:] = m_new

    def values(slot, blk, qlo=0):
        vt = vt_scr[blk]
        for c in range(2):
            acc_scr[c, :, qlo:] = (a_scr[slot, c, :, qlo:] * acc_scr[c, :, qlo:]
                                   + jnp.dot(vt, p_scr[slot, c, :, qlo:], preferred_element_type=F32))

    scores(2 * i + 1, 0, diag=1, qlo=tk)
    scores(2 * i, 1, diag=0)
    softmax(0, qlo=tk)
    scores(0, 0)
    softmax(1)
    values(0, 2 * i + 1, tk)

    def pair(j, last):
        scores(2 * j - 1, 1)
        softmax(0)
        values(1, jnp.where(j == 1, 2 * i, 2 * j - 3))
        if not last:
            scores(2 * j, 0)
        softmax(1)
        values(0, 2 * j - 2)

    def body(j, carry):
        pair(j, False)
        return carry

    lax.fori_loop(1, i, body, 0)

    @pl.when(i >= 1)
    def _():
        pair(i, True)

    values(1, jnp.where(i == 0, 0, 2 * i - 1))

    lam_v = lam_ref[...]
    lam = (jnp.exp(jnp.sum(lam_v[0:1] * lam_v[1:2], axis=-1, keepdims=True))
           - jnp.exp(jnp.sum(lam_v[2:3] * lam_v[3:4], axis=-1, keepdims=True)) + lambda_init)
    o = (acc_scr[0, 0:dv, :] * (1.0 / acc_scr[0, dv:dv + 1, :])
         - acc_scr[1, 0:dv, :] * (lam / acc_scr[1, dv:dv + 1, :]))
    ms = jnp.mean(o * o, axis=0, keepdims=True)
    y = o * lax.rsqrt(ms + EPS) * gs_ref[...] * (1.0 - lambda_init)
    o_ref[...] = y.T.astype(o_ref.dtype)


def _diff_attn(qk, v, lam_vecs, g_subln, batch, seq, heads, dk, lambda_init, tq, rider):
    dv = 2 * dk
    tk = tq // 2
    nq = seq // tq
    kernel = functools.partial(_diff_attn_kernel, tq=tq, tk=tk, dk=dk, lambda_init=lambda_init,
                               rider_args=(rider.nblocks, rider.body))
    r_in, r_out, r_shape = _rider_io(rider, lambda b, h, i: (b * heads + h) * nq + i, batch * heads * nq)
    return pl.pallas_call(
        kernel,
        grid=(batch, heads, nq),
        in_specs=[pl.BlockSpec((tq, dv), lambda b, h, i: (b * nq + i, h)),
                  pl.BlockSpec((seq, dv), lambda b, h, i: (b, heads + h)),
                  pl.BlockSpec((seq, dv), lambda b, h, i: (b, h)),
                  pl.BlockSpec((4, dk), lambda b, h, i: (0, 0)),
                  pl.BlockSpec((dv, 1), lambda b, h, i: (0, 0)), r_in],
        out_specs=[pl.BlockSpec((tq, dv), lambda b, h, i: (b * nq + i, h)), r_out],
        out_shape=[jax.ShapeDtypeStruct((batch * seq, heads * dv), BF16), r_shape],
        scratch_shapes=[pltpu.VMEM((seq // tk, dv + SUM_ROWS, tk), BF16),
                        pltpu.VMEM((2, 2, tk, tq), F32), pltpu.VMEM((2, 2, tk, tq), BF16),
                        pltpu.VMEM((2, 2, 1, tq), F32), pltpu.VMEM((2, 2, 1, tq), F32),
                        pltpu.VMEM((2, 1, tq), F32), pltpu.VMEM((2, dv + SUM_ROWS, tq), F32)],
        compiler_params=_params("arbitrary", "arbitrary", "arbitrary"),
        name="diff_attn",
    )(qk, qk, v, lam_vecs, g_subln.reshape(dv, 1), rider.src)


def _pool_kernel(up_ref, prev_ref, wp_ref, ps_ref, o_ref, ext_scr, *, tm, nseq, halo, gdim):
    i = pl.program_id(0)
    ti = i % nseq
    ext_scr[halo:halo + tm, :] = up_ref[...]
    ext_scr[0:halo, :] = jnp.where(ti == 0, 0.0, prev_ref[...])
    t1 = (ti * tm + 1 + lax.broadcasted_iota(jnp.int32, (tm, 1), 0)).astype(F32)
    for g, win in enumerate(POOL_WINDOWS):
        sl = slice(g * gdim, (g + 1) * gdim)
        cur = ext_scr[halo:halo + tm, sl]
        tot = cur
        for d in range(1, win):
            tot = tot + ext_scr[halo - d:halo - d + tm, sl]
        y = tot / jnp.minimum(t1, float(win)) - cur
        z = jnp.dot(y.astype(BF16), wp_ref[g], preferred_element_type=F32) * ps_ref[g:g + 1, :]
        o_ref[:, sl] = z.astype(o_ref.dtype)


def _pool_mixer(up, w_pool, pool_scale, seq, tm):
    m, width = up.shape
    groups, gdim, _ = w_pool.shape
    halo = 2 * SUBLANES
    assert max(POOL_WINDOWS) <= halo
    nseq = seq // tm
    per = tm // halo
    kernel = functools.partial(_pool_kernel, tm=tm, nseq=nseq, halo=halo, gdim=gdim)
    return pl.pallas_call(
        kernel,
        grid=(m // tm,),
        in_specs=[pl.BlockSpec((tm, width), lambda i: (i, 0)),
                  pl.BlockSpec((halo, width), lambda i: (jnp.maximum(i * per - 1, 0), 0)),
                  pl.BlockSpec((groups, gdim, gdim), lambda i: (0, 0, 0)),
                  pl.BlockSpec((groups, gdim), lambda i: (0, 0))],
        out_specs=pl.BlockSpec((tm, width), lambda i: (i, 0)),
        out_shape=jax.ShapeDtypeStruct((m, width), BF16),
        scratch_shapes=[pltpu.VMEM((tm + halo, width), F32)],
        compiler_params=_params("parallel"),
        name="pool_mixer",
    )(up, up, w_pool, pool_scale)


def _mem_attn_kernel(q_ref, k_ref, v_ref, o_ref, *, heads, hd):
    scale = hd ** -0.5
    for h in range(heads):
        sl = slice(h * hd, (h + 1) * hd)
        s = lax.dot_general(q_ref[:, sl], k_ref[:, sl], (((1,), (1,)), ((), ())),
                            preferred_element_type=F32) * scale
        p = jnp.exp(s - jnp.max(s, axis=-1, keepdims=True))
        l = jnp.sum(p, axis=-1, keepdims=True)
        o = jnp.dot(p.astype(BF16), v_ref[:, sl], preferred_element_type=F32) / l
        o_ref[:, sl] = o.astype(o_ref.dtype)


def _mem_attn(qm, km, vm, batch, seq, mem_len, heads, tq):
    m, width = qm.shape
    nq = seq // tq
    kernel = functools.partial(_mem_attn_kernel, heads=heads, hd=width // heads)
    return pl.pallas_call(
        kernel,
        grid=(batch, nq),
        in_specs=[pl.BlockSpec((tq, width), lambda b, i: (b * nq + i, 0)),
                  pl.BlockSpec((mem_len, width), lambda b, i: (b, 0)),
                  pl.BlockSpec((mem_len, width), lambda b, i: (b, 0))],
        out_specs=pl.BlockSpec((tq, width), lambda b, i: (b * nq + i, 0)),
        out_shape=jax.ShapeDtypeStruct((m, width), BF16),
        compiler_params=_params("parallel", "parallel"),
        name="mem_attn",
    )(qm, km, vm)


def _merge_kernel(ya_ref, yb_ref, yc_ref, wa_ref, wb_ref, wc_ref, ga_ref, gb_ref, gc_ref, o_ref):
    out = jax.nn.sigmoid(ga_ref[...]) * jnp.dot(ya_ref[...], wa_ref[...], preferred_element_type=F32)
    out = out + jax.nn.sigmoid(gb_ref[...]) * jnp.dot(yb_ref[...], wb_ref[...],
                                                      preferred_element_type=F32)
    out = out + jax.nn.sigmoid(gc_ref[...]) * jnp.dot(yc_ref[...], wc_ref[...],
                                                      preferred_element_type=F32)
    o_ref[...] = out.astype(o_ref.dtype)


def _merge(ya, yb, yc, w_a, w_b, w_c, gl, tm, tn):
    m = ya.shape[0]
    d = w_a.shape[1]
    nj = d // tn
    row = lambda a: pl.BlockSpec((tm, a.shape[1]), lambda i, j: (i, 0))
    col = lambda w: pl.BlockSpec((w.shape[0], tn), lambda i, j: (0, j))
    gate = lambda b: pl.BlockSpec((tm, tn), lambda i, j: (i, b * nj + j))
    return pl.pallas_call(
        _merge_kernel,
        grid=(m // tm, nj),
        in_specs=[row(ya), row(yb), row(yc), col(w_a), col(w_b), col(w_c), gate(0), gate(1), gate(2)],
        out_specs=pl.BlockSpec((tm, tn), lambda i, j: (i, j)),
        out_shape=jax.ShapeDtypeStruct((m, d), BF16),
        compiler_params=_params("parallel", "parallel"),
        name="merge",
    )(ya, yb, yc, w_a, w_b, w_c, gl, gl, gl)


def _proj_residual_kernel(a_ref, w_ref, r_ref, o_ref):
    o_ref[...] = r_ref[...] + jnp.dot(a_ref[...], w_ref[...], preferred_element_type=F32)


def _proj_residual(a, w, res, tm, tn):
    m, k = a.shape
    n = w.shape[1]
    return pl.pallas_call(
        _proj_residual_kernel,
        grid=(m // tm, n // tn),
        in_specs=[pl.BlockSpec((tm, k), lambda i, j: (i, 0)),
                  pl.BlockSpec((k, tn), lambda i, j: (0, j)),
                  pl.BlockSpec((tm, tn), lambda i, j: (i, j))],
        out_specs=pl.BlockSpec((tm, tn), lambda i, j: (i, j)),
        out_shape=jax.ShapeDtypeStruct((m, n), F32),
        compiler_params=_params("parallel", "parallel"),
        name="proj_residual",
    )(a, w, res)


def _ffn_up_kernel(h_ref, wg_ref, wv_ref, cwg_ref, cwv_ref, cbg_ref, cbv_ref, o_ref, yg_scr, yv_scr,
                   *, tm, nseq):
    i = pl.program_id(1)
    pad = SUBLANES

    @pl.when(i % nseq == 0)
    def _():
        yg_scr[0:pad, :] = jnp.zeros((pad, yg_scr.shape[1]), F32)
        yv_scr[0:pad, :] = jnp.zeros((pad, yv_scr.shape[1]), F32)

    def conv(lo, cols, y_scr, cw_ref, cb_ref):
        u = cb_ref[:, cols] + y_scr[lo:lo + ROW_CHUNK, cols] * cw_ref[CONV_W - 1:CONV_W, cols]
        for d in range(1, CONV_W):
            u = u + y_scr[lo - d:lo - d + ROW_CHUNK, cols] * cw_ref[CONV_W - 1 - d:CONV_W - d, cols]
        return u

    for r in range(tm // ROW_CHUNK):
        rows = slice(r * ROW_CHUNK, (r + 1) * ROW_CHUNK)
        lo = pad + r * ROW_CHUNK
        yg_scr[lo:lo + ROW_CHUNK, :] = jnp.dot(h_ref[rows, :], wg_ref[...], preferred_element_type=F32)
        yv_scr[lo:lo + ROW_CHUNK, :] = jnp.dot(h_ref[rows, :], wv_ref[...], preferred_element_type=F32)
        for c in range(o_ref.shape[1] // LANES):
            cols = slice(c * LANES, (c + 1) * LANES)
            gate = conv(lo, cols, yg_scr, cwg_ref, cbg_ref)
            val = conv(lo, cols, yv_scr, cwv_ref, cbv_ref)
            o_ref[rows, cols] = (gate * jax.nn.sigmoid(gate) * val).astype(o_ref.dtype)
    yg_scr[0:pad, :] = yg_scr[tm:tm + pad, :]
    yv_scr[0:pad, :] = yv_scr[tm:tm + pad, :]


def _ffn_up(h, w_up, conv_w, conv_b, seq, tm, tn):
    m, k = h.shape
    ff = w_up.shape[1] // 2
    nj = ff // tn
    nseq = seq // tm
    kernel = functools.partial(_ffn_up_kernel, tm=tm, nseq=nseq)
    wspec = lambda half: pl.BlockSpec((k, tn), lambda j, i: (0, half * nj + j))
    cwspec = lambda half: pl.BlockSpec((CONV_W, tn), lambda j, i: (0, half * nj + j))
    cbspec = lambda half: pl.BlockSpec((1, tn), lambda j, i: (0, half * nj + j))
    scratch = pltpu.VMEM((tm + SUBLANES, tn), F32)
    return pl.pallas_call(
        kernel,
        grid=(nj, m // tm),
        in_specs=[pl.BlockSpec((tm, k), lambda j, i: (i, 0)),
                  wspec(0), wspec(1), cwspec(0), cwspec(1), cbspec(0), cbspec(1)],
        out_specs=pl.BlockSpec((tm, tn), lambda j, i: (i, j)),
        out_shape=jax.ShapeDtypeStruct((m, ff), BF16),
        scratch_shapes=[scratch, scratch],
        compiler_params=_params("arbitrary", "arbitrary"),
        name="ffn_up",
    )(h, w_up, w_up, conv_w, conv_w, conv_b.reshape(1, -1), conv_b.reshape(1, -1))


def _ffn_down_kernel(a_ref, w_ref, r_ref, o_ref, acc_scr, *, rem):
    kk = pl.program_id(2)
    last = pl.num_programs(2) - 1

    @pl.when(kk == 0)
    def _():
        acc_scr[...] = r_ref[...]

    @pl.when(kk < last)
    def _():
        acc_scr[...] += jnp.dot(a_ref[...], w_ref[...], preferred_element_type=F32)

    @pl.when(kk == last)
    def _():
        o_ref[...] = acc_scr[...] + jnp.dot(a_ref[:, :rem], w_ref[:rem, :], preferred_element_type=F32)


def _ffn_down(a, w, res, tm, tn, tk):
    m = a.shape[0]
    k, n = w.shape
    nk = pl.cdiv(k, tk)
    rem = k - (nk - 1) * tk
    assert tk % LANES == 0 and rem % LANES == 0 and a.shape[1] >= nk * tk
    return pl.pallas_call(
        functools.partial(_ffn_down_kernel, rem=rem),
        grid=(m // tm, n // tn, nk),
        in_specs=[pl.BlockSpec((tm, tk), lambda i, j, kk: (i, kk)),
                  pl.BlockSpec((tk, tn), lambda i, j, kk: (kk, j)),
                  pl.BlockSpec((tm, tn), lambda i, j, kk: (i, j))],
        out_specs=pl.BlockSpec((tm, tn), lambda i, j, kk: (i, j)),
        out_shape=jax.ShapeDtypeStruct((m, n), F32),
        scratch_shapes=[pltpu.VMEM((tm, tn), F32)],
        compiler_params=_params("parallel", "parallel", "arbitrary"),
        name="ffn_down",
    )(a, w, res)


def _rope_tables(seq, rope_dim):
    half = rope_dim // 2
    pos = jnp.arange(seq, dtype=F32)
    inv = ROPE_THETA ** (-jnp.arange(0, rope_dim, 2, dtype=F32) / rope_dim)
    ang = pos[:, None] * inv[None, :]
    cos, sin = jnp.cos(ang), jnp.sin(ang)
    ones = jnp.ones((seq, LANES - rope_dim), F32)
    zeros = jnp.zeros((seq, LANES - rope_dim), F32)
    zh = jnp.zeros((seq, half), F32)
    cos_t = jnp.concatenate([cos, cos, ones], axis=1)
    sa_t = jnp.concatenate([-sin, zh, zeros], axis=1)
    sb_t = jnp.concatenate([zh, sin, zeros], axis=1)
    return cos_t, sa_t, sb_t


def _pad_halves(a, ff, ff_pad):
    pad = [(0, 0)] * (a.ndim - 1) + [(0, ff_pad - ff)]
    return jnp.concatenate([jnp.pad(a[..., :ff], pad), jnp.pad(a[..., ff:], pad)], axis=-1)


def kernel(x, mem, g_attn_norm, w_in, g_qa, g_ka, lam_q1, lam_k1, lam_q2, lam_k2, g_subln, w_pool,
           pool_scale, g_mem, w_mkv, g_qm, g_km, w_a, w_b, w_c, w_o, g_ffn_norm, w_up, conv_w, conv_b,
           w_down):
    batch, seq, d_model = x.shape
    mem_len = mem.shape[1]
    depth = w_in.shape[0]
    dk = g_qa.shape[1]
    dv = g_subln.shape[1]
    heads = w_a.shape[1] // dv
    qk_width = heads * 2 * dk
    pool_width = w_b.shape[1]
    mem_width = w_c.shape[1]
    mem_hd = g_qm.shape[1]
    rope_dim = dk // 4
    d_ff = w_down.shape[1]
    ff_pad = -(-d_ff // FF_ALIGN) * FF_ALIGN
    assert dk == LANES and dv == 2 * dk
    tm, tn, tnh = ROW_TILE, COL_TILE, HALF_COL_TILE

    m = batch * seq
    xf = x.reshape(m, d_model)
    cos_t, sa_t, sb_t = _rope_tables(seq, rope_dim)
    c_v = 2 * qk_width
    c_up = c_v + heads * dv
    c_qm = c_up + pool_width
    c_gl = c_qm + mem_width

    for l in range(depth):
        lambda_init = 0.8 - 0.6 * math.exp(-0.3 * l)
        g_qk = jnp.concatenate([jnp.tile(g_qa[l] * (dk ** -0.5 * LOG2E), qk_width // dk),
                                jnp.tile(g_ka[l], qk_width // dk)])
        lam_vecs = jnp.stack([lam_q1[l], lam_k1[l], lam_q2[l], lam_k2[l]]).astype(F32)

        w_qk_b = w_in[l][:, :c_v].astype(BF16)
        nrest = (w_in.shape[2] - c_v) // c_v
        rr = W_IN_RIDER_ROWS
        rest_rider = _Rider(w_in[l], (rr, c_v), lambda n: (n // nrest, 1 + n % nrest),
                            (d_model, nrest * c_v), (rr, c_v), lambda n: (n // nrest, n % nrest),
                            (d_model // rr) * nrest, _cast_body)

        h = _rmsnorm(xf, g_attn_norm[l], NORM_ROWS)
        qk, w_rest_b = _proj_qk(h, w_qk_b, g_qk, cos_t, sa_t, sb_t, seq, rope_dim, tm, tn, rest_rider)
        v = _proj_plain(h, w_rest_b, 0, heads * dv, BF16, tm, tn, "proj_v")
        up = _proj_plain(h, w_rest_b, c_up - c_v, pool_width, F32, tm, tn, "proj_up")
        qm = _proj_norm(h, w_rest_b, g_qm[l], c_qm - c_v, mem_width, tm, tn, "proj_qm")
        gl, w_down_b = _proj_plain(h, w_rest_b, c_gl - c_v, 3 * d_model, F32, tm, tn, "proj_gates",
                                   rider=_row_block_rider(w_down[l], W_DOWN_RIDER_ROWS, d_model, _cast_body))

        ya, w_up_b = _diff_attn(qk, v, lam_vecs, g_subln[l], batch, seq, heads, dk, lambda_init, ATTN_TILE,
                                _row_block_rider(w_up[l], W_UP_RIDER_ROWS, 2 * ff_pad, _pad_halves_body))
        yb = _pool_mixer(up, w_pool[l].astype(BF16), pool_scale[l], seq, SIDE_TILE)

        mem_rows = batch * mem_len
        mem_n = _rmsnorm(mem.reshape(mem_rows, d_model), g_mem[l], NORM_ROWS)
        w_mkv_b = w_mkv[l].astype(BF16)
        km = _proj_norm(mem_n, w_mkv_b, g_km[l], 0, mem_width, mem_rows, tn, "proj_km")
        vm = _proj_plain(mem_n, w_mkv_b, mem_width, mem_width, BF16, mem_rows, tn, "proj_vm")
        yc = _mem_attn(qm, km, vm, batch, seq, mem_len, mem_width // mem_hd, SIDE_TILE)

        merged = _merge(ya, yb, yc, w_a[l].astype(BF16), w_b[l].astype(BF16), w_c[l].astype(BF16),
                        gl, tm, tnh)
        xf = _proj_residual(merged, w_o[l].astype(BF16), xf, tm, tnh)

        h2 = _rmsnorm(xf, g_ffn_norm[l], NORM_ROWS)
        act = _ffn_up(h2, w_up_b, _pad_halves(conv_w[l], d_ff, ff_pad),
                      _pad_halves(conv_b[l], d_ff, ff_pad), seq, tm, tnh)
        xf = _ffn_down(act, w_down_b, xf, tm, tn, ff_pad // FFN_DOWN_K_STEPS)

    return xf.reshape(batch, seq, d_model)
```
